```python
import jax, jax.numpy as jnp
from jax import lax
import numpy as np

D_MODEL = 2048
BATCH = 2
SEQ = 8192
DEPTH = 1
DEC_BATCH = 32
DEC_SEQ = 1
PAST_LEN = 16384
PAGE_SIZE = 128

D_MIX = D_MODEL
C_CONV = D_MIX // 2
CONV_WIDTH = 31
HEAD_DIM = 64
D_ATTN = D_MIX - C_CONV
N_HEADS = D_ATTN // HEAD_DIM
N_KV_HEADS = 4
KV_GROUP = N_HEADS // N_KV_HEADS
H_IDX = 16
D_IDX = 64
TOPK_MAX = 256
Q_BLOCK = 128
D_FF = ((8 * D_MODEL // 3 + 127) // 128) * 128
EPS = 1e-6
ATTN_SCALE = HEAD_DIM ** -0.5
IDX_SCALE = (D_IDX ** -0.5) * (H_IDX ** -0.5)
SPLIT_SIZES = [2 * C_CONV, D_ATTN, N_KV_HEADS * HEAD_DIM, N_KV_HEADS * HEAD_DIM, H_IDX * D_IDX, D_IDX, H_IDX]
SPLIT_POINTS = [int(v) for v in np.cumsum(SPLIT_SIZES)[:-1]]
D_IN = int(sum(SPLIT_SIZES))

kernel_name = "hymba_conformer_dsa_step"


def rmsnorm(x, g):
    xf = x.astype(jnp.float32)
    y = xf * lax.rsqrt(jnp.mean(xf * xf, axis=-1, keepdims=True) + EPS)
    return (y * g.astype(jnp.float32)).astype(x.dtype)


def swiglu(x, w_in, w_out):
    a, b = jnp.split(x @ w_in, 2, axis=-1)
    return (jax.nn.silu(a) * b) @ w_out


def combined_projection(h, w_in):
    B, T = h.shape[:2]
    conv_a, conv_g, = None, None
    p_conv, q, k, v, q_idx, k_idx, w_idx = jnp.split(h @ w_in, SPLIT_POINTS, axis=-1)
    conv_a, conv_g = jnp.split(p_conv, 2, axis=-1)
    u = conv_a * jax.nn.sigmoid(conv_g)
    q = q.reshape(B, T, N_HEADS, HEAD_DIM)
    k = k.reshape(B, T, N_KV_HEADS, HEAD_DIM)
    v = v.reshape(B, T, N_KV_HEADS, HEAD_DIM)
    q_idx = q_idx.reshape(B, T, H_IDX, D_IDX)
    return u, q, k, v, q_idx, k_idx, w_idx


def conv_module(u, conv_prev, conv_w, conv_b, ln_g, ln_b):
    xpad = jnp.concatenate([conv_prev, u], axis=1)
    y = lax.conv_general_dilated(xpad, conv_w[:, None, :].astype(xpad.dtype), window_strides=(1,),
                                 padding='VALID', dimension_numbers=('NWC', 'WIO', 'NWC'),
                                 feature_group_count=C_CONV) + conv_b
    yf = y.astype(jnp.float32)
    mu = jnp.mean(yf, axis=-1, keepdims=True)
    var = jnp.mean(jnp.square(yf - mu), axis=-1, keepdims=True)
    yn = (yf - mu) * lax.rsqrt(var + EPS) * ln_g.astype(jnp.float32) + ln_b.astype(jnp.float32)
    out = jax.nn.silu(yn).astype(u.dtype)
    return out, xpad[:, -(CONV_WIDTH - 1):, :]


def select_keys(q_idx, w_idx, k_idx, q_pos, n_sel):
    s = jax.nn.relu(jnp.einsum('bthd,bsd->bths', q_idx.astype(jnp.float32), k_idx.astype(jnp.float32)))
    score = jnp.einsum('bths,bth->bts', s, w_idx.astype(jnp.float32)) * IDX_SCALE
    k_pos = jnp.arange(k_idx.shape[1])
    admissible = k_pos[None, :] <= q_pos[:, None]
    score = jnp.where(admissible[None], score, -jnp.inf)
    _, idx = lax.top_k(score, n_sel)
    valid = idx <= q_pos[None, :, None]
    return idx, valid


def gather_rows(a, idx):
    return jax.vmap(lambda a_b, i_b: a_b[i_b])(a, idx)


def sparse_attend(q, k_sel, v_sel, valid):
    B, T = q.shape[:2]
    qg = q.reshape(B, T, N_KV_HEADS, KV_GROUP, HEAD_DIM)
    logits = jnp.einsum('btngd,btknd->btngk', qg.astype(jnp.float32), k_sel.astype(jnp.float32)) * ATTN_SCALE
    logits = jnp.where(valid[:, :, None, None, :], logits, jnp.finfo(jnp.float32).min)
    p = jax.nn.softmax(logits, axis=-1).astype(v_sel.dtype)
    o = jnp.einsum('btngk,btknd->btngd', p, v_sel)
    return o.reshape(B, T, D_ATTN)


def prompt_attention(q, k, v, q_idx, k_idx, w_idx):
    B, S = q.shape[:2]
    n_sel = min(TOPK_MAX, S // 4)
    nb = S // Q_BLOCK

    def blocks(a):
        return jnp.moveaxis(a.reshape((B, nb, Q_BLOCK) + a.shape[2:]), 1, 0)

    def one_block(args):
        q_b, qi_b, w_b, start = args
        q_pos = start + jnp.arange(Q_BLOCK)
        idx, valid = select_keys(qi_b, w_b, k_idx, q_pos, n_sel)
        return sparse_attend(q_b, gather_rows(k, idx), gather_rows(v, idx), valid)

    out = lax.map(one_block, (blocks(q), blocks(q_idx), blocks(w_idx), jnp.arange(nb) * Q_BLOCK))
    return jnp.moveaxis(out, 0, 1).reshape(B, S, D_ATTN)


def sample_attention(q, k, v, q_idx, k_idx, w_idx, cache_k, cache_v, cache_idx_k, page_table):
    DB, T = q.shape[:2]
    past = page_table.shape[1] * PAGE_SIZE
    n_sel = min(TOPK_MAX, (past + T) // 4)
    k_idx_past = cache_idx_k[page_table].reshape(DB, past, D_IDX)
    k_idx_all = jnp.concatenate([k_idx_past, k_idx.astype(k_idx_past.dtype)], axis=1)
    q_pos = past + jnp.arange(T)
    idx, valid = select_keys(q_idx, w_idx, k_idx_all, q_pos, n_sel)
    from_past = idx < past
    pidx = jnp.minimum(idx, past - 1)
    phys = jnp.take_along_axis(page_table, (pidx // PAGE_SIZE).reshape(DB, -1), axis=1).reshape(pidx.shape)
    slot = pidx % PAGE_SIZE
    nidx = jnp.clip(idx - past, 0, T - 1)
    sel = from_past[..., None, None]
    k_sel = jnp.where(sel, cache_k[phys, slot].astype(k.dtype), gather_rows(k, nidx))
    v_sel = jnp.where(sel, cache_v[phys, slot].astype(v.dtype), gather_rows(v, nidx))
    return sparse_attend(q, k_sel, v_sel, valid)


def decoder_layer(x, conv_prev, attend, norm_ffn1, ffn1_w_in, ffn1_w_out, norm_mix, w_in,
                  conv_w, conv_b, conv_ln_g, conv_ln_b, w_out, norm_ffn2, ffn2_w_in, ffn2_w_out):
    x = x + 0.5 * swiglu(rmsnorm(x, norm_ffn1), ffn1_w_in, ffn1_w_out)
    h = rmsnorm(x, norm_mix)
    u, q, k, v, q_idx, k_idx, w_idx = combined_projection(h, w_in)
    conv_out, conv_new = conv_module(u, conv_prev, conv_w, conv_b, conv_ln_g, conv_ln_b)
    attn_out = attend(q, k, v, q_idx, k_idx, w_idx)
    x = x + jnp.concatenate([conv_out, attn_out], axis=-1) @ w_out
    x = x + 0.5 * swiglu(rmsnorm(x, norm_ffn2), ffn2_w_in, ffn2_w_out)
    return x, k, v, k_idx, conv_new


def setup_inputs(seed: int = 0) -> dict:
    key = jax.random.key(seed)
    ks = jax.random.split(key, 24)
    f32 = jnp.float32
    n_pages = PAST_LEN // PAGE_SIZE
    n_pool = (5 * DEC_BATCH * n_pages) // 4
    nrm = lambda k, shape, s: jax.random.normal(k, shape, f32) * s
    gain = lambda k, shape: 1.0 + 0.01 * jax.random.normal(k, shape, f32)
    page_table = jax.random.permutation(ks[6], n_pool)[:DEC_BATCH * n_pages].reshape(DEC_BATCH, n_pages).astype(jnp.int32)
    return {
        "x_prompt": nrm(ks[0], (BATCH, SEQ, D_MODEL), 1.0),
        "x_sample": nrm(ks[1], (DEC_BATCH, DEC_SEQ, D_MODEL), 1.0),
        "cache_k": nrm(ks[2], (DEPTH, n_pool, PAGE_SIZE, N_KV_HEADS, HEAD_DIM), 1.0),
        "cache_v": nrm(ks[3], (DEPTH, n_pool, PAGE_SIZE, N_KV_HEADS, HEAD_DIM), 1.0),
        "cache_idx_k": nrm(ks[4], (DEPTH, n_pool, PAGE_SIZE, D_IDX), 1.0),
        "state_conv": nrm(ks[5], (DEPTH, DEC_BATCH, CONV_WIDTH - 1, C_CONV), 0.5),
        "page_table": page_table,
        "norm_ffn1": gain(ks[7], (DEPTH, D_MODEL)),
        "ffn1_w_in": nrm(ks[8], (DEPTH, D_MODEL, 2 * D_FF), D_MODEL ** -0.5),
        "ffn1_w_out": nrm(ks[9], (DEPTH, D_FF, D_MODEL), D_FF ** -0.5),
        "norm_mix": gain(ks[10], (DEPTH, D_MODEL)),
        "w_in": nrm(ks[11], (DEPTH, D_MODEL, D_IN), D_MODEL ** -0.5),
        "conv_w": nrm(ks[12], (DEPTH, CONV_WIDTH, C_CONV), CONV_WIDTH ** -0.5),
        "conv_b": nrm(ks[13], (DEPTH, C_CONV), 0.01),
        "conv_ln_g": gain(ks[14], (DEPTH, C_CONV)),
        "conv_ln_b": nrm(ks[15], (DEPTH, C_CONV), 0.01),
        "w_out": nrm(ks[16], (DEPTH, D_MIX, D_MODEL), D_MIX ** -0.5),
        "norm_ffn2": gain(ks[17], (DEPTH, D_MODEL)),
        "ffn2_w_in": nrm(ks[18], (DEPTH, D_MODEL, 2 * D_FF), D_MODEL ** -0.5),
        "ffn2_w_out": nrm(ks[19], (DEPTH, D_FF, D_MODEL), D_FF ** -0.5),
        "norm_final": gain(ks[20], (D_MODEL,)),
    }


def reference(x_prompt, x_sample, cache_k, cache_v, cache_idx_k, state_conv, page_table,
              norm_ffn1, ffn1_w_in, ffn1_w_out, norm_mix, w_in, conv_w, conv_b, conv_ln_g, conv_ln_b,
              w_out, norm_ffn2, ffn2_w_in, ffn2_w_out, norm_final):
    xp, xs = x_prompt, x_sample
    kp, vp, ikp, cp = [], [], [], []
    ksm, vsm, iks, cs = [], [], [], []
    for l in range(DEPTH):
        weights = (norm_ffn1[l], ffn1_w_in[l], ffn1_w_out[l], norm_mix[l], w_in[l], conv_w[l], conv_b[l],
                   conv_ln_g[l], conv_ln_b[l], w_out[l], norm_ffn2[l], ffn2_w_in[l], ffn2_w_out[l])
        conv0 = jnp.zeros((xp.shape[0], CONV_WIDTH - 1, C_CONV), xp.dtype)
        xp, k_new, v_new, ik_new, c_new = decoder_layer(xp, conv0, prompt_attention, *weights)
        kp.append(k_new); vp.append(v_new); ikp.append(ik_new); cp.append(c_new)
        attend_s = (lambda ck, cv, cik: (lambda q, k, v, qi, ki, wi: sample_attention(q, k, v, qi, ki, wi, ck, cv, cik, page_table)))(cache_k[l], cache_v[l], cache_idx_k[l])
        xs, k_new, v_new, ik_new, c_new = decoder_layer(xs, state_conv[l].astype(xs.dtype), attend_s, *weights)
        ksm.append(k_new); vsm.append(v_new); iks.append(ik_new); cs.append(c_new)
    y_prompt = rmsnorm(xp, norm_final)
    y_sample = rmsnorm(xs, norm_final)
    return (y_prompt, y_sample,
            jnp.stack(kp), jnp.stack(vp), jnp.stack(ikp), jnp.stack(cp),
            jnp.stack(ksm), jnp.stack(vsm), jnp.stack(iks), jnp.stack(cs))
```

```python
import functools

import jax
import jax.numpy as jnp
from jax import lax
from jax.experimental import pallas as pl
from jax.experimental.pallas import tpu as pltpu

F32 = jnp.float32
BF16 = jnp.bfloat16
I32 = jnp.int32

C_CONV_FRACTION = 2
CONV_WIDTH = 31
HEAD_DIM = 64
N_KV_HEADS = 4
H_IDX = 16
D_IDX = 64
TOPK_MAX = 256
EPS = 1e-6
ATTN_SCALE = HEAD_DIM ** -0.5
IDX_SCALE = (D_IDX ** -0.5) * (H_IDX ** -0.5)

LANES = 128
SUBLANES = 8
VMEM_LIMIT_BYTES = 56 * 1024 * 1024

FFN_TM = 512
FFN_TF = 512
PROJ_TM = 256
CONV_HALO = 32
ATT_TQ = 128
ATT_CK_SCORE = 256
ATT_CK = 512

INT_MIN = -2 ** 31
NEG_BIG = -1e30
F32_LOWEST = float(jnp.finfo(jnp.float32).min)


def _round_up(x, m):
    return (x + m - 1) // m * m


def _cparams(sem):
    return pltpu.CompilerParams(dimension_semantics=sem, vmem_limit_bytes=VMEM_LIMIT_BYTES)


def _resident(shape, index_map):
    return pl.BlockSpec(shape, index_map, pipeline_mode=pl.Buffered(1))


def _rms(x, g):
    ms = jnp.mean(x * x, axis=-1, keepdims=True)
    return x * lax.rsqrt(ms + EPS) * g


def _dot(a, b):
    return jnp.dot(a, b, preferred_element_type=F32)


def _dot_nt(a, b):
    return lax.dot_general(a, b, (((1,), (1,)), ((), ())), preferred_element_type=F32)


def _ffn_kernel(*refs, has_mix, has_final):
    it = iter(refs)
    x_ref = next(it)
    if has_mix:
        mc_ref, ma_ref, woc_ref, woa_ref = next(it), next(it), next(it), next(it)
    g_ref, wa_ref, wb_ref, wo_ref = next(it), next(it), next(it), next(it)
    if has_final:
        gf_ref = next(it)
    o_ref = next(it)
    xn_ref = next(it)

    f = pl.program_id(1)

    @pl.when(f == 0)
    def _():
        x = x_ref[...]
        if has_mix:
            x = x + _dot(mc_ref[...], woc_ref[...]) + _dot(ma_ref[...], woa_ref[...])
        o_ref[...] = x
        xn_ref[...] = _rms(x, g_ref[...]).astype(BF16)

    xn = xn_ref[...]
    a = _dot(xn, wa_ref[...])
    b = _dot(xn, wb_ref[...])
    act = (a * jax.nn.sigmoid(a) * b).astype(BF16)
    o_ref[...] += 0.5 * _dot(act, wo_ref[...])

    if has_final:
        @pl.when(f == pl.num_programs(1) - 1)
        def _():
            o_ref[...] = _rms(o_ref[...], gf_ref[...])


def _ffn(x, g, wa, wb, wo, *, mix=None, g_final=None, name):
    m, d = x.shape
    fp = wa.shape[1]
    tm = min(FFN_TM, m)
    assert m % tm == 0 and fp % FFN_TF == 0
    nf = fp // FFN_TF
    has_mix = mix is not None
    has_final = g_final is not None

    row = lambda i, f: (i, 0)
    const = lambda i, f: (0, 0)
    args = [x]
    specs = [pl.BlockSpec((tm, d), row)]
    if has_mix:
        mc, ma, woc, woa = mix
        args += [mc, ma, woc, woa]
        specs += [pl.BlockSpec((tm, mc.shape[1]), row), pl.BlockSpec((tm, ma.shape[1]), row),
                  _resident(woc.shape, const), _resident(woa.shape, const)]
    args += [g, wa, wb, wo]
    specs += [_resident((1, d), const),
              pl.BlockSpec((d, FFN_TF), lambda i, f: (0, f)),
              pl.BlockSpec((d, FFN_TF), lambda i, f: (0, f)),
              pl.BlockSpec((FFN_TF, d), lambda i, f: (f, 0))]
    if has_final:
        args.append(g_final)
        specs.append(_resident((1, d), const))

    return pl.pallas_call(
        functools.partial(_ffn_kernel, has_mix=has_mix, has_final=has_final),
        grid=(m // tm, nf),
        in_specs=specs,
        out_specs=pl.BlockSpec((tm, d), row),
        out_shape=jax.ShapeDtypeStruct((m, d), F32),
        scratch_shapes=[pltpu.VMEM((tm, d), BF16)],
        compiler_params=_cparams(("arbitrary", "arbitrary")),
        name=name,
    )(*args)


def _proj_layout(d_mix):
    c_conv = d_mix // C_CONV_FRACTION
    d_attn = d_mix - c_conv
    n_heads = d_attn // HEAD_DIM
    widths = dict(ca=c_conv, cg=c_conv, qpad=n_heads * LANES, k=N_KV_HEADS * HEAD_DIM,
                  v=N_KV_HEADS * HEAD_DIM, qi=H_IDX * D_IDX, kia=LANES, kib=LANES, wi=LANES)
    off, o = {}, 0
    for name, w in widths.items():
        off[name] = (o, o + w)
        o += w
    return off, o


def _proj_kernel(*refs, tm, tiles_per_seq, sample, d_mix):
    it = iter(refs)
    x_ref, g_ref, w_ref, cw_ref, cp_ref = next(it), next(it), next(it), next(it), next(it)
    if sample:
        st_ref = next(it)
    (conv_ref, qpad_ref, k_ref, v_ref, kb_ref, vb_ref, qi_ref, kia_ref, kib_ref, ki_ref, wi_ref,
     u_ref) = (next(it) for _ in range(12))
    if not sample:
        win_ref, y_ref = next(it), next(it)

    off, _ = _proj_layout(d_mix)
    c_conv = d_mix // C_CONV_FRACTION

    def cols(name):
        lo, hi = off[name]
        return w_ref[:, lo:hi]

    xn = _rms(x_ref[...], g_ref[...]).astype(BF16)

    qpad_ref[...] = (_dot(xn, cols("qpad")) * ATTN_SCALE).astype(BF16)
    kk = _dot(xn, cols("k"))
    vv = _dot(xn, cols("v"))
    k_ref[...] = kk
    v_ref[...] = vv
    kb_ref[...] = kk.astype(BF16)
    vb_ref[...] = vv.astype(BF16)
    qi_ref[...] = _dot(xn, cols("qi")).astype(BF16)
    kia = _dot(xn, cols("kia"))
    kia_ref[...] = kia.astype(BF16)
    kib_ref[...] = _dot(xn, cols("kib")).astype(BF16)
    ki_ref[...] = kia[:, :D_IDX]
    wi_ref[...] = _dot(xn, cols("wi"))[:, :H_IDX] * IDX_SCALE

    u = _dot(xn, cols("ca")) * jax.nn.sigmoid(_dot(xn, cols("cg")))
    bias = cp_ref[0:1, :]
    ln_g = cp_ref[1:2, :]
    ln_b = cp_ref[2:3, :]

    if sample:
        u_ref[...] = u
        y = bias + cw_ref[CONV_WIDTH - 1:CONV_WIDTH, :] * u
        for j in range(CONV_WIDTH - 1):
            y = y + cw_ref[j:j + 1, :] * st_ref[j]
    else:
        @pl.when(pl.program_id(0) % tiles_per_seq == 0)
        def _():
            win_ref[0:CONV_HALO, :] = jnp.zeros((CONV_HALO, c_conv), F32)

        win_ref[CONV_HALO:CONV_HALO + tm, :] = u
        first = CONV_HALO - (CONV_WIDTH - 1)
        for c in range(c_conv // LANES):
            cs = slice(c * LANES, (c + 1) * LANES)
            acc = jnp.zeros((tm, LANES), F32) + bias[:, cs]
            for j in range(CONV_WIDTH):
                acc = acc + cw_ref[j:j + 1, cs] * win_ref[first + j:first + j + tm, cs]
            y_ref[:, cs] = acc
        y = y_ref[...]
        tail = win_ref[tm:tm + CONV_HALO, :]
        u_ref[0] = tail
        win_ref[0:CONV_HALO, :] = tail

    mu = jnp.mean(y, axis=-1, keepdims=True)
    var = jnp.mean(jnp.square(y - mu), axis=-1, keepdims=True)
    yn = (y - mu) * lax.rsqrt(var + EPS) * ln_g + ln_b
    conv_ref[...] = (yn * jax.nn.sigmoid(yn)).astype(BF16)


def _proj(x, g, w_all, conv_w, conv_p, *, seq_len, state=None, name):
    m, d = x.shape
    d_mix = d
    c_conv = d_mix // C_CONV_FRACTION
    n_heads = (d_mix - c_conv) // HEAD_DIM
    kvw = N_KV_HEADS * HEAD_DIM
    sample = state is not None
    tm = m if sample else min(PROJ_TM, seq_len)
    assert m % tm == 0 and seq_len % tm == 0 or sample
    assert tm >= CONV_HALO or sample
    nt = m // tm
    tiles_per_seq = max(seq_len // tm, 1)
    n_seq = m // seq_len

    row = lambda i: (i, 0)
    const = lambda i: (0, 0)
    args = [x, g, w_all, conv_w, conv_p]
    specs = [pl.BlockSpec((tm, d), row), _resident((1, d), const), _resident(w_all.shape, const),
             _resident(conv_w.shape, const), _resident(conv_p.shape, const)]
    if sample:
        args.append(state)
        specs.append(_resident(state.shape, lambda i: (0, 0, 0)))

    def out(width, dtype):
        return jax.ShapeDtypeStruct((m, width), dtype), pl.BlockSpec((tm, width), row)

    outs = [out(c_conv, BF16), out(n_heads * LANES, BF16), out(kvw, F32), out(kvw, F32),
            out(kvw, BF16), out(kvw, BF16), out(H_IDX * D_IDX, BF16), out(LANES, BF16),
            out(LANES, BF16), out(D_IDX, F32), out(H_IDX, F32)]
    if sample:
        outs.append(out(c_conv, F32))
        scratch = []
    else:
        outs.append((jax.ShapeDtypeStruct((n_seq, CONV_HALO, c_conv), F32),
                     pl.BlockSpec((1, CONV_HALO, c_conv), lambda i: (i // tiles_per_seq, 0, 0))))
        scratch = [pltpu.VMEM((tm + CONV_HALO, c_conv), F32), pltpu.VMEM((tm, c_conv), F32)]

    return pl.pallas_call(
        functools.partial(_proj_kernel, tm=tm, tiles_per_seq=tiles_per_seq, sample=sample, d_mix=d_mix),
        grid=(nt,),
        in_specs=specs,
        out_specs=[o[1] for o in outs],
        out_shape=[o[0] for o in outs],
        scratch_shapes=scratch,
        compiler_params=_cparams(("arbitrary",)),
        name=name,
    )(*args)


def _key_to_f32(key):
    bits = key ^ ((key >> 31) & jnp.int32(0x7FFFFFFF))
    return lax.bitcast_convert_type(bits, F32)


def _count(sc_ref, nchunks, ck, pred):
    rows = sc_ref.shape[0]

    def body(c, acc):
        start = pl.multiple_of(c * ck, ck)
        hit = jnp.where(pred(sc_ref[:, pl.ds(start, ck)], start), 1.0, 0.0)
        part = hit[:, 0:LANES]
        for i in range(1, ck // LANES):
            part = part + hit[:, i * LANES:(i + 1) * LANES]
        return acc + part

    acc = lax.fori_loop(0, nchunks, body, jnp.zeros((rows, LANES), F32))
    return jnp.sum(acc, axis=1, keepdims=True)


def _select_threshold(sc_ref, nchunks, ck, n_sel, n_adm, total_cols):
    rows = sc_ref.shape[0]
    want = jnp.float32(n_sel)

    def bisect(i, carry):
        t, ct = carry
        cand = t + lax.shift_left(jnp.int32(1), 31 - i)
        thr = _key_to_f32(cand)
        cnt = _count(sc_ref, nchunks, ck, lambda blk, _: blk >= thr)
        take = cnt >= want
        return jnp.where(take, cand, t), jnp.where(take, cnt, ct)

    t, ct = lax.fori_loop(0, 32, bisect, (jnp.full((rows, 1), INT_MIN, I32), jnp.zeros((rows, 1), F32)))
    full = n_adm <= n_sel
    thr = jnp.where(full, F32_LOWEST, _key_to_f32(t))
    tied = jnp.logical_and(jnp.logical_not(full), ct > want)

    @pl.when(jnp.sum(jnp.where(tied, 1.0, 0.0)) > 0.0)
    def _():
        n_gt = _count(sc_ref, nchunks, ck, lambda blk, _: blk > thr)
        room = want - n_gt

        def col_index(shape, start):
            return start + lax.broadcasted_iota(I32, shape, 1)

        nbits = max(int(total_cols - 1).bit_length(), 1)

        def search(i, q):
            cand = q + lax.shift_left(jnp.int32(1), nbits - 1 - i)
            below = _count(sc_ref, nchunks, ck,
                           lambda blk, s: jnp.logical_and(blk == thr, col_index(blk.shape, s) < cand))
            return jnp.where(below < room, cand, q)

        last = lax.fori_loop(0, nbits, search, jnp.zeros((rows, 1), I32))

        def drop(c, _):
            start = pl.multiple_of(c * ck, ck)
            blk = sc_ref[:, pl.ds(start, ck)]
            lose = jnp.logical_and(tied, jnp.logical_and(blk == thr, col_index(blk.shape, start) > last))
            sc_ref[:, pl.ds(start, ck)] = jnp.where(lose, -jnp.inf, blk)
            return 0

        lax.fori_loop(0, nchunks, drop, 0)

    return thr


def _attn_kernel(qi_ref, w_ref, kia_ref, kib_ref, q_ref, k_ref, v_ref, o_ref, sc_ref, *, tq, n_sel, seq_len):
    t0 = pl.program_id(1) * tq
    nk = t0 // ATT_CK + 1
    row_pos = t0 + lax.broadcasted_iota(I32, (tq, 1), 0)
    n_heads = q_ref.shape[2] // LANES
    group = n_heads // N_KV_HEADS

    def score(c, _):
        start = pl.multiple_of(c * ATT_CK_SCORE, ATT_CK_SCORE)
        ka = kia_ref[0, pl.ds(start, ATT_CK_SCORE), :]
        kb = kib_ref[0, pl.ds(start, ATT_CK_SCORE), :]
        acc = jnp.zeros((tq, ATT_CK_SCORE), F32)
        for j in range(H_IDX // 2):
            pair = qi_ref[0, :, j * LANES:(j + 1) * LANES]
            acc = acc + jnp.maximum(_dot_nt(pair, ka), 0.0) * w_ref[0, :, 2 * j:2 * j + 1]
            acc = acc + jnp.maximum(_dot_nt(pair, kb), 0.0) * w_ref[0, :, 2 * j + 1:2 * j + 2]
        col = start + lax.broadcasted_iota(I32, (tq, ATT_CK_SCORE), 1)
        sc_ref[:, pl.ds(start, ATT_CK_SCORE)] = jnp.where(col <= row_pos, acc, -jnp.inf)
        return 0

    lax.fori_loop(0, nk * (ATT_CK // ATT_CK_SCORE), score, 0)

    thr = _select_threshold(sc_ref, nk, ATT_CK, n_sel, row_pos + 1, seq_len)

    for n in range(N_KV_HEADS):
        lanes = slice((n // 2) * LANES, (n // 2 + 1) * LANES)
        qg = jnp.concatenate([q_ref[0, :, (group * n + g) * LANES:(group * n + g + 1) * LANES]
                              for g in range(group)], axis=0)

        def attend(c, carry):
            m_i, l_i, acc = carry
            start = pl.multiple_of(c * ATT_CK, ATT_CK)
            drop = jnp.where(sc_ref[:, pl.ds(start, ATT_CK)] >= thr, 0.0, NEG_BIG)
            s = _dot_nt(qg, k_ref[0, pl.ds(start, ATT_CK), lanes]) + jnp.concatenate([drop] * group, axis=0)
            m_new = jnp.maximum(m_i, jnp.max(s, axis=1, keepdims=True))
            alpha = jnp.exp(m_i - m_new)
            p = jnp.exp(s - m_new)
            l_new = alpha * l_i + jnp.sum(p, axis=1, keepdims=True)
            acc = alpha * acc + _dot(p.astype(BF16), v_ref[0, pl.ds(start, ATT_CK), lanes])
            return m_new, l_new, acc

        init = (jnp.full((group * tq, 1), NEG_BIG, F32), jnp.zeros((group * tq, 1), F32),
                jnp.zeros((group * tq, LANES), F32))
        _, l_i, acc = lax.fori_loop(0, nk, attend, init)
        out = acc / l_i
        half = slice((n % 2) * HEAD_DIM, (n % 2 + 1) * HEAD_DIM)
        o_ref[0, :, n * group * HEAD_DIM:(n + 1) * group * HEAD_DIM] = jnp.concatenate(
            [out[g * tq:(g + 1) * tq, half] for g in range(group)], axis=1).astype(o_ref.dtype)


def _prompt_attention(qi, wi, kia, kib, qpad, kb, vb, *, n_sel):
    b, s, _ = qi.shape
    tq = min(ATT_TQ, s)
    assert s % ATT_CK == 0 and ATT_CK % tq == 0 and ATT_CK % ATT_CK_SCORE == 0
    n_heads = qpad.shape[2] // LANES
    tile = lambda bi, i: (bi, i, 0)
    seq = lambda bi, i: (bi, 0, 0)
    return pl.pallas_call(
        functools.partial(_attn_kernel, tq=tq, n_sel=n_sel, seq_len=s),
        grid=(b, s // tq),
        in_specs=[pl.BlockSpec((1, tq, qi.shape[2]), tile), pl.BlockSpec((1, tq, wi.shape[2]), tile),
                  _resident((1, s, LANES), seq), _resident((1, s, LANES), seq),
                  pl.BlockSpec((1, tq, qpad.shape[2]), tile),
                  _resident((1, s, kb.shape[2]), seq), _resident((1, s, vb.shape[2]), seq)],
        out_specs=pl.BlockSpec((1, tq, n_heads * HEAD_DIM), tile),
        out_shape=jax.ShapeDtypeStruct((b, s, n_heads * HEAD_DIM), BF16),
        scratch_shapes=[pltpu.VMEM((tq, s), F32)],
        compiler_params=_cparams(("arbitrary", "arbitrary")),
        name="prompt_attention",
    )(qi, wi, kia, kib, qpad, kb, vb)


def _sample_select_kernel(pt_ref, qi_ref, w_ref, kin_ref, cache_ref, idx_ref, buf_ref, row_ref, sc_ref, sem,
                          *, n_pages, page, n_sel, rows):
    b = pl.program_id(0)
    past = n_pages * page
    total = rows * LANES

    def page_copy(j):
        return pltpu.make_async_copy(cache_ref.at[pt_ref[b, j]], buf_ref.at[pl.ds(j * page, page)], sem)

    def start(j, _):
        page_copy(j).start()
        return 0

    def wait(j, _):
        page_copy(j).wait()
        return 0

    lax.fori_loop(0, n_pages, start, 0)

    @pl.when(b == 0)
    def _():
        buf_ref[pl.ds(past, total - past), :] = jnp.zeros((total - past, D_IDX), F32)

    lax.fori_loop(0, n_pages, wait, 0)
    buf_ref[pl.ds(past, 1), :] = kin_ref[0]

    x = _dot_nt(qi_ref[0], buf_ref[...].astype(BF16))
    score = jnp.sum(jnp.maximum(x, 0.0) * w_ref[0], axis=0, keepdims=True)
    col = lax.broadcasted_iota(I32, (1, total), 1)
    row_ref[...] = jnp.where(col <= past, score, -jnp.inf)
    for r in range(rows):
        sc_ref[r:r + 1, :] = row_ref[:, r * LANES:(r + 1) * LANES]

    want = jnp.float32(n_sel)

    def count(pred):
        return jnp.sum(jnp.sum(jnp.where(pred, 1.0, 0.0), axis=1, keepdims=True), axis=0, keepdims=True)

    sc = sc_ref[...]

    def bisect(i, carry):
        t, ct = carry
        cand = t + lax.shift_left(jnp.int32(1), 31 - i)
        cnt = count(sc >= _key_to_f32(cand))
        take = cnt >= want
        return jnp.where(take, cand, t), jnp.where(take, cnt, ct)

    t, ct = lax.fori_loop(0, 32, bisect, (jnp.full((1, 1), INT_MIN, I32), jnp.zeros((1, 1), F32)))
    thr = _key_to_f32(t)
    pos = lax.broadcasted_iota(I32, (rows, LANES), 0) * LANES + lax.broadcasted_iota(I32, (rows, LANES), 1)
    tie = sc == thr
    room = want - count(sc > thr)
    nbits = max(int(total - 1).bit_length(), 1)

    def search(i, q):
        cand = q + lax.shift_left(jnp.int32(1), nbits - 1 - i)
        below = count(jnp.logical_and(tie, pos < cand))
        return jnp.where(below < room, cand, q)

    last = lax.fori_loop(0, nbits, search, jnp.zeros((1, 1), I32))
    keep = jnp.logical_or(sc > thr, jnp.logical_and(tie, pos <= last))

    kb16 = jnp.where(keep, 1.0, 0.0).astype(BF16)
    li = lax.broadcasted_iota(I32, (LANES, LANES), 0)
    lj = lax.broadcasted_iota(I32, (LANES, LANES), 1)
    within = _dot(kb16, jnp.where(li <= lj, 1.0, 0.0).astype(BF16))
    rowsum = within[:, LANES - 1:LANES]
    ri = lax.broadcasted_iota(I32, (rows, rows), 0)
    rj = lax.broadcasted_iota(I32, (rows, rows), 1)
    incl = _dot(jnp.where(rj <= ri, 1.0, 0.0).astype(BF16),
                jnp.broadcast_to(rowsum, (rows, LANES)).astype(BF16))[:, 0:1]
    slot = lax.broadcasted_iota(I32, (rows, n_sel), 1).astype(F32)
    row_of = jnp.sum(jnp.where(incl <= slot, 1.0, 0.0), axis=0, keepdims=True)
    onehot = lax.broadcasted_iota(I32, (rows, n_sel), 0).astype(F32) == row_of
    row_off = jnp.sum(jnp.where(onehot, incl - rowsum, 0.0), axis=0, keepdims=True)
    rank = slot[0:1, :] - row_off + 1.0
    marked = jnp.where(keep, within, 0.0).astype(BF16)
    ranks_t = lax.dot_general(marked, jnp.where(onehot, 1.0, 0.0).astype(BF16),
                              (((0,), (0,)), ((), ())), preferred_element_type=F32)
    lane_id = lax.broadcasted_iota(I32, (LANES, n_sel), 0).astype(F32)
    lane_of = jnp.sum(jnp.where(ranks_t == rank, lane_id, 0.0), axis=0, keepdims=True)
    idx_ref[0] = (row_of * LANES + lane_of).astype(I32)


def _sample_select(page_table, qi, wi, ki_new, cache_idx, *, n_sel):
    db, n_pages = page_table.shape
    page = cache_idx.shape[1]
    past = n_pages * page
    total = _round_up(past + 1, SUBLANES * LANES)
    rows = total // LANES
    grid_spec = pltpu.PrefetchScalarGridSpec(
        num_scalar_prefetch=1,
        grid=(db,),
        in_specs=[pl.BlockSpec((1, H_IDX, D_IDX), lambda b, pt: (b, 0, 0)),
                  pl.BlockSpec((1, H_IDX, 1), lambda b, pt: (b, 0, 0)),
                  pl.BlockSpec((1, 1, D_IDX), lambda b, pt: (b, 0, 0)),
                  pl.BlockSpec(memory_space=pl.ANY)],
        out_specs=pl.BlockSpec((1, 1, n_sel), lambda b, pt: (b, 0, 0)),
        scratch_shapes=[pltpu.VMEM((total, D_IDX), F32), pltpu.VMEM((1, total), F32),
                        pltpu.VMEM((rows, LANES), F32), pltpu.SemaphoreType.DMA(())],
    )
    return pl.pallas_call(
        functools.partial(_sample_select_kernel, n_pages=n_pages, page=page, n_sel=n_sel, rows=rows),
        grid_spec=grid_spec,
        out_shape=jax.ShapeDtypeStruct((db, 1, n_sel), I32),
        compiler_params=_cparams(("arbitrary",)),
        name="sample_select",
    )(page_table, qi, wi, ki_new, cache_idx)


def _sample_attend_kernel(idx_ref, pt_ref, q_ref, knew_ref, vnew_ref, ck_ref, cv_ref, o_ref, kbuf, vbuf, sem,
                          *, n_pages, page, n_sel):
    b = pl.program_id(0)
    past = n_pages * page

    def cached_rows(j, i):
        p = jnp.minimum(i, past - 1)
        phys = pt_ref[b, p // page]
        slot = p % page
        return (pltpu.make_async_copy(ck_ref.at[phys, pl.ds(slot, 1)], kbuf.at[pl.ds(j, 1)], sem),
                pltpu.make_async_copy(cv_ref.at[phys, pl.ds(slot, 1)], vbuf.at[pl.ds(j, 1)], sem))

    def start(j, _):
        i = idx_ref[b, j]

        @pl.when(i < past)
        def _():
            for cp in cached_rows(j, i):
                cp.start()

        @pl.when(i >= past)
        def _():
            pltpu.make_async_copy(knew_ref.at[b], kbuf.at[pl.ds(j, 1)], sem).start()
            pltpu.make_async_copy(vnew_ref.at[b], vbuf.at[pl.ds(j, 1)], sem).start()
        return 0

    def wait(j, _):
        for cp in cached_rows(j, idx_ref[b, j]):
            cp.wait()
        return 0

    lax.fori_loop(0, n_sel, start, 0)
    lax.fori_loop(0, n_sel, wait, 0)

    q = q_ref[0]
    kb = kbuf[...].astype(BF16)
    vb = vbuf[...].astype(BF16)
    n_heads = q.shape[0]
    first_pair = lax.broadcasted_iota(I32, (n_heads, 1), 0) < n_heads // 2
    s = jnp.where(first_pair, _dot_nt(q, kb[:, 0:LANES]), _dot_nt(q, kb[:, LANES:2 * LANES]))
    p = jnp.exp(s - jnp.max(s, axis=1, keepdims=True))
    p = (p / jnp.sum(p, axis=1, keepdims=True)).astype(BF16)
    o_ref[0] = jnp.where(first_pair, _dot(p, vb[:, 0:LANES]), _dot(p, vb[:, LANES:2 * LANES]))


def _sample_attend(idx, page_table, q, k_new, v_new, cache_k, cache_v, *, n_sel):
    db, n_pages = page_table.shape
    page, kvw = cache_k.shape[1], cache_k.shape[2]
    n_heads = q.shape[1]
    assert kvw == 2 * LANES and N_KV_HEADS == 4
    grid_spec = pltpu.PrefetchScalarGridSpec(
        num_scalar_prefetch=2,
        grid=(db,),
        in_specs=[pl.BlockSpec((1, n_heads, LANES), lambda b, ix, pt: (b, 0, 0)),
                  pl.BlockSpec(memory_space=pl.ANY), pl.BlockSpec(memory_space=pl.ANY),
                  pl.BlockSpec(memory_space=pl.ANY), pl.BlockSpec(memory_space=pl.ANY)],
        out_specs=pl.BlockSpec((1, n_heads, LANES), lambda b, ix, pt: (b, 0, 0)),
        scratch_shapes=[pltpu.VMEM((n_sel, kvw), F32), pltpu.VMEM((n_sel, kvw), F32),
                        pltpu.SemaphoreType.DMA(())],
    )
    return pl.pallas_call(
        functools.partial(_sample_attend_kernel, n_pages=n_pages, page=page, n_sel=n_sel),
        grid_spec=grid_spec,
        out_shape=jax.ShapeDtypeStruct((db, n_heads, LANES), F32),
        compiler_params=_cparams(("arbitrary",)),
        name="sample_attend",
    )(idx, page_table, q, k_new, v_new, cache_k, cache_v)


def _pack_ffn(w_in, w_out):
    d, f2 = w_in.shape
    f = f2 // 2
    fp = _round_up(f, FFN_TF)
    pad = lambda w, axis: jnp.pad(w, [(0, fp - f) if a == axis else (0, 0) for a in range(2)])
    return (pad(w_in[:, :f], 1).astype(BF16), pad(w_in[:, f:], 1).astype(BF16), pad(w_out, 0).astype(BF16))


def _pack_proj(w_in, d_mix):
    d = w_in.shape[0]
    c_conv = d_mix // C_CONV_FRACTION
    d_attn = d_mix - c_conv
    n_heads = d_attn // HEAD_DIM
    kvw = N_KV_HEADS * HEAD_DIM
    sizes = [2 * c_conv, d_attn, kvw, kvw, H_IDX * D_IDX, D_IDX, H_IDX]
    parts, o = [], 0
    for sz in sizes:
        parts.append(w_in[:, o:o + sz])
        o += sz
    p_conv, q, k, v, qi, ki, wi = parts
    qh = q.reshape(d, n_heads, HEAD_DIM)
    zero = jnp.zeros_like(qh)
    odd = ((jnp.arange(n_heads) // (n_heads // N_KV_HEADS)) % 2 == 1)[None, :, None]
    qpad = jnp.concatenate([jnp.where(odd, zero, qh), jnp.where(odd, qh, zero)], axis=-1).reshape(d, n_heads * LANES)
    z = jnp.zeros((d, LANES - D_IDX), w_in.dtype)
    cols = [p_conv[:, :c_conv], p_conv[:, c_conv:], qpad, k, v, qi,
            jnp.concatenate([ki, z], axis=1), jnp.concatenate([z, ki], axis=1),
            jnp.concatenate([wi, jnp.zeros((d, LANES - H_IDX), w_in.dtype)], axis=1)]
    w_all = jnp.concatenate(cols, axis=1).astype(BF16)
    assert w_all.shape[1] == _proj_layout(d_mix)[1]
    return w_all


def _own_half(o_pad, n_heads):
    odd = ((jnp.arange(n_heads) // (n_heads // N_KV_HEADS)) % 2 == 1)[:, None]
    return jnp.where(odd, o_pad[..., HEAD_DIM:], o_pad[..., :HEAD_DIM])


def kernel(x_prompt, x_sample, cache_k, cache_v, cache_idx_k, state_conv, page_table, norm_ffn1, ffn1_w_in, ffn1_w_out, norm_mix, w_in, conv_w, conv_b, conv_ln_g, conv_ln_b, w_out, norm_ffn2, ffn2_w_in, ffn2_w_out, norm_final):
    depth = w_in.shape[0]
    bsz, seq, d = x_prompt.shape
    db, dseq, _ = x_sample.shape
    assert dseq == 1
    d_mix = d
    c_conv = d_mix // C_CONV_FRACTION
    n_heads = (d_mix - c_conv) // HEAD_DIM
    kvw = N_KV_HEADS * HEAD_DIM
    n_pool, page = cache_k.shape[1], cache_k.shape[2]
    n_pages = page_table.shape[1]
    past = n_pages * page
    n_sel_p = min(TOPK_MAX, seq // 4)
    n_sel_s = min(TOPK_MAX, (past + dseq) // 4)

    xp = x_prompt.reshape(bsz * seq, d)
    xs = x_sample.reshape(db, d)
    g_final = norm_final.reshape(1, d)
    outs_p, outs_s = [], []

    for l in range(depth):
        f1 = _pack_ffn(ffn1_w_in[l], ffn1_w_out[l])
        f2 = _pack_ffn(ffn2_w_in[l], ffn2_w_out[l])
        w_all = _pack_proj(w_in[l], d_mix)
        conv_p = jnp.stack([conv_b[l], conv_ln_g[l], conv_ln_b[l]])
        wo = w_out[l].astype(BF16)
        woc, woa = wo[:c_conv], wo[c_conv:]
        g1, gm, g2 = norm_ffn1[l].reshape(1, d), norm_mix[l].reshape(1, d), norm_ffn2[l].reshape(1, d)
        last = g_final if l == depth - 1 else None

        xp = _ffn(xp, g1, *f1, name="ffn1_prompt")
        (conv_o, qpad, k_new, v_new, kb, vb, qi, kia, kib, ki_new, wi, u_tail) = _proj(
            xp, gm, w_all, conv_w[l], conv_p, seq_len=seq, name="proj_prompt")
        r3 = lambda a: a.reshape(bsz, seq, a.shape[-1])
        attn = _prompt_attention(r3(qi), r3(wi), r3(kia), r3(kib), r3(qpad), r3(kb), r3(vb), n_sel=n_sel_p)
        xp = _ffn(xp, g2, *f2, mix=(conv_o, attn.reshape(bsz * seq, -1), woc, woa), g_final=last,
                  name="ffn2_prompt")
        outs_p.append((k_new.reshape(bsz, seq, N_KV_HEADS, HEAD_DIM), v_new.reshape(bsz, seq, N_KV_HEADS, HEAD_DIM),
                       ki_new.reshape(bsz, seq, D_IDX), u_tail[:, CONV_HALO - (CONV_WIDTH - 1):, :]))

        state = state_conv[l].astype(F32)
        xs = _ffn(xs, g1, *f1, name="ffn1_sample")
        (conv_o, qpad, k_new, v_new, _, _, qi, _, _, ki_new, wi, u_new) = _proj(
            xs, gm, w_all, conv_w[l], conv_p, seq_len=1, state=jnp.swapaxes(state, 0, 1), name="proj_sample")
        idx = _sample_select(page_table, qi.reshape(db, H_IDX, D_IDX), wi.reshape(db, H_IDX, 1),
                             ki_new.reshape(db, 1, D_IDX), cache_idx_k[l], n_sel=n_sel_s)
        o_pad = _sample_attend(idx.reshape(db, n_sel_s), page_table, qpad.reshape(db, n_heads, LANES),
                               k_new.reshape(db, 1, kvw), v_new.reshape(db, 1, kvw),
                               cache_k[l].reshape(n_pool, page, kvw), cache_v[l].reshape(n_pool, page, kvw),
                               n_sel=n_sel_s)
        attn = _own_half(o_pad, n_heads).reshape(db, n_heads * HEAD_DIM).astype(BF16)
        xs = _ffn(xs, g2, *f2, mix=(conv_o, attn, woc, woa), g_final=last, name="ffn2_sample")
        outs_s.append((k_new.reshape(db, 1, N_KV_HEADS, HEAD_DIM), v_new.reshape(db, 1, N_KV_HEADS, HEAD_DIM),
                       ki_new.reshape(db, 1, D_IDX),
                       jnp.concatenate([state[:, 1:, :], u_new[:, None, :]], axis=1)))

    stack = lambda outs, i: jnp.stack([o[i] for o in outs])
    return (xp.reshape(bsz, seq, d), xs.reshape(db, 1, d),
            stack(outs_p, 0), stack(outs_p, 1), stack(outs_p, 2), stack(outs_p, 3),
            stack(outs_s, 0), stack(outs_s, 1), stack(outs_s, 2), stack(outs_s, 3))
```

```python
import functools

import jax
import jax.numpy as jnp
from jax import lax
from jax.experimental import pallas as pl
from jax.experimental.pallas import tpu as pltpu

F32 = jnp.float32
BF16 = jnp.bfloat16
I32 = jnp.int32

C_CONV_FRACTION = 2
CONV_WIDTH = 31
HEAD_DIM = 64
N_KV_HEADS = 4
H_IDX = 16
D_IDX = 64
TOPK_MAX = 256
EPS = 1e-6
ATTN_SCALE = HEAD_DIM ** -0.5
IDX_SCALE = (D_IDX ** -0.5) * (H_IDX ** -0.5)

LANES = 128
SUBLANES = 8
VMEM_LIMIT_BYTES = 56 * 1024 * 1024

FFN_TM = 512
FFN_TF = 512
PROJ_TM = 256
CONV_HALO = 32
ATT_TQ = 128
ATT_TH = 128
ATT_CK_SCORE = 256
ATT_CK = 512
ATT_UNROLL = 4
SEARCH_CAP = 40

INT_MIN = -2 ** 31
NEG_BIG = -1e30
F32_LOWEST = float(jnp.finfo(jnp.float32).min)


def _round_up(x, m):
    return (x + m - 1) // m * m


def _cparams(sem):
    return pltpu.CompilerParams(dimension_semantics=sem, vmem_limit_bytes=VMEM_LIMIT_BYTES)


def _resident(shape, index_map):
    return pl.BlockSpec(shape, index_map, pipeline_mode=pl.Buffered(1))


def _rms(x, g):
    ms = jnp.mean(x * x, axis=-1, keepdims=True)
    return x * lax.rsqrt(ms + EPS) * g


def _dot(a, b):
    return jnp.dot(a, b, preferred_element_type=F32)


def _dot_nt(a, b):
    return lax.dot_general(a, b, (((1,), (1,)), ((), ())), preferred_element_type=F32)


def _ffn_kernel(*refs, has_mix, has_final):
    it = iter(refs)
    x_ref = next(it)
    if has_mix:
        mc_ref, ma_ref, woc_ref, woa_ref = next(it), next(it), next(it), next(it)
    g_ref, wa_ref, wb_ref, wo_ref = next(it), next(it), next(it), next(it)
    if has_final:
        gf_ref = next(it)
    o_ref = next(it)
    xn_ref = next(it)

    f = pl.program_id(1)

    @pl.when(f == 0)
    def _():
        x = x_ref[...]
        if has_mix:
            x = x + _dot(mc_ref[...], woc_ref[...]) + _dot(ma_ref[...], woa_ref[...])
        o_ref[...] = x
        xn_ref[...] = _rms(x, g_ref[...]).astype(BF16)

    xn = xn_ref[...]
    a = _dot(xn, wa_ref[...])
    b = _dot(xn, wb_ref[...])
    act = (a * jax.nn.sigmoid(a) * b).astype(BF16)
    o_ref[...] += 0.5 * _dot(act, wo_ref[...])

    if has_final:
        @pl.when(f == pl.num_programs(1) - 1)
        def _():
            o_ref[...] = _rms(o_ref[...], gf_ref[...])


def _ffn(x, g, wa, wb, wo, *, mix=None, g_final=None, name):
    m, d = x.shape
    fp = wa.shape[1]
    tm = min(FFN_TM, m)
    assert m % tm == 0 and fp % FFN_TF == 0
    nf = fp // FFN_TF
    has_mix = mix is not None
    has_final = g_final is not None

    row = lambda i, f: (i, 0)
    const = lambda i, f: (0, 0)
    args = [x]
    specs = [pl.BlockSpec((tm, d), row)]
    if has_mix:
        mc, ma, woc, woa = mix
        args += [mc, ma, woc, woa]
        specs += [pl.BlockSpec((tm, mc.shape[1]), row), pl.BlockSpec((tm, ma.shape[1]), row),
                  _resident(woc.shape, const), _resident(woa.shape, const)]
    args += [g, wa, wb, wo]
    specs += [_resident((1, d), const),
              pl.BlockSpec((d, FFN_TF), lambda i, f: (0, f)),
              pl.BlockSpec((d, FFN_TF), lambda i, f: (0, f)),
              pl.BlockSpec((FFN_TF, d), lambda i, f: (f, 0))]
    if has_final:
        args.append(g_final)
        specs.append(_resident((1, d), const))

    return pl.pallas_call(
        functools.partial(_ffn_kernel, has_mix=has_mix, has_final=has_final),
        grid=(m // tm, nf),
        in_specs=specs,
        out_specs=pl.BlockSpec((tm, d), row),
        out_shape=jax.ShapeDtypeStruct((m, d), F32),
        scratch_shapes=[pltpu.VMEM((tm, d), BF16)],
        compiler_params=_cparams(("arbitrary", "arbitrary")),
        name=name,
    )(*args)


def _proj_layout(d_mix):
    c_conv = d_mix // C_CONV_FRACTION
    d_attn = d_mix - c_conv
    n_heads = d_attn // HEAD_DIM
    widths = dict(ca=c_conv, cg=c_conv, qpad=n_heads * LANES, k=N_KV_HEADS * HEAD_DIM,
                  v=N_KV_HEADS * HEAD_DIM, qi=H_IDX * D_IDX, kia=LANES, kib=LANES, wi=LANES)
    off, o = {}, 0
    for name, w in widths.items():
        off[name] = (o, o + w)
        o += w
    return off, o


def _proj_kernel(*refs, tm, tiles_per_seq, sample, d_mix):
    it = iter(refs)
    x_ref, g_ref, w_ref, cw_ref, cp_ref = next(it), next(it), next(it), next(it), next(it)
    if sample:
        st_ref = next(it)
    (conv_ref, qpad_ref, k_ref, v_ref, kb_ref, vb_ref, qi_ref, kia_ref, kib_ref, ki_ref, wi_ref,
     u_ref) = (next(it) for _ in range(12))
    if not sample:
        win_ref, y_ref = next(it), next(it)

    off, _ = _proj_layout(d_mix)
    c_conv = d_mix // C_CONV_FRACTION

    def cols(name):
        lo, hi = off[name]
        return w_ref[:, lo:hi]

    xn = _rms(x_ref[...], g_ref[...]).astype(BF16)

    qpad_ref[...] = (_dot(xn, cols("qpad")) * ATTN_SCALE).astype(BF16)
    kk = _dot(xn, cols("k"))
    vv = _dot(xn, cols("v"))
    k_ref[...] = kk
    v_ref[...] = vv
    kb_ref[...] = kk.astype(BF16)
    vb_ref[...] = vv.astype(BF16)
    qi_ref[...] = _dot(xn, cols("qi")).astype(BF16)
    kia = _dot(xn, cols("kia"))
    kia_ref[...] = kia.astype(BF16)
    kib_ref[...] = _dot(xn, cols("kib")).astype(BF16)
    ki_ref[...] = kia[:, :D_IDX]
    wi_ref[...] = _dot(xn, cols("wi"))[:, :H_IDX] * IDX_SCALE

    u = _dot(xn, cols("ca")) * jax.nn.sigmoid(_dot(xn, cols("cg")))
    bias = cp_ref[0:1, :]
    ln_g = cp_ref[1:2, :]
    ln_b = cp_ref[2:3, :]

    if sample:
        u_ref[...] = u
        y = bias + cw_ref[CONV_WIDTH - 1:CONV_WIDTH, :] * u
        for j in range(CONV_WIDTH - 1):
            y = y + cw_ref[j:j + 1, :] * st_ref[j]
    else:
        @pl.when(pl.program_id(0) % tiles_per_seq == 0)
        def _():
            win_ref[0:CONV_HALO, :] = jnp.zeros((CONV_HALO, c_conv), F32)

        win_ref[CONV_HALO:CONV_HALO + tm, :] = u
        first = CONV_HALO - (CONV_WIDTH - 1)
        for c in range(c_conv // LANES):
            cs = slice(c * LANES, (c + 1) * LANES)
            acc = jnp.zeros((tm, LANES), F32) + bias[:, cs]
            for j in range(CONV_WIDTH):
                acc = acc + cw_ref[j:j + 1, cs] * win_ref[first + j:first + j + tm, cs]
            y_ref[:, cs] = acc
        y = y_ref[...]
        tail = win_ref[tm:tm + CONV_HALO, :]
        u_ref[0] = tail
        win_ref[0:CONV_HALO, :] = tail

    mu = jnp.mean(y, axis=-1, keepdims=True)
    var = jnp.mean(jnp.square(y - mu), axis=-1, keepdims=True)
    yn = (y - mu) * lax.rsqrt(var + EPS) * ln_g + ln_b
    conv_ref[...] = (yn * jax.nn.sigmoid(yn)).astype(BF16)


def _proj(x, g, w_all, conv_w, conv_p, *, seq_len, state=None, name):
    m, d = x.shape
    d_mix = d
    c_conv = d_mix // C_CONV_FRACTION
    n_heads = (d_mix - c_conv) // HEAD_DIM
    kvw = N_KV_HEADS * HEAD_DIM
    sample = state is not None
    tm = m if sample else min(PROJ_TM, seq_len)
    assert m % tm == 0 and seq_len % tm == 0 or sample
    assert tm >= CONV_HALO or sample
    nt = m // tm
    tiles_per_seq = max(seq_len // tm, 1)
    n_seq = m // seq_len

    row = lambda i: (i, 0)
    const = lambda i: (0, 0)
    args = [x, g, w_all, conv_w, conv_p]
    specs = [pl.BlockSpec((tm, d), row), _resident((1, d), const), _resident(w_all.shape, const),
             _resident(conv_w.shape, const), _resident(conv_p.shape, const)]
    if sample:
        args.append(state)
        specs.append(_resident(state.shape, lambda i: (0, 0, 0)))

    def out(width, dtype):
        return jax.ShapeDtypeStruct((m, width), dtype), pl.BlockSpec((tm, width), row)

    outs = [out(c_conv, BF16), out(n_heads * LANES, BF16), out(kvw, F32), out(kvw, F32),
            out(kvw, BF16), out(kvw, BF16), out(H_IDX * D_IDX, BF16), out(LANES, BF16),
            out(LANES, BF16), out(D_IDX, F32), out(H_IDX, F32)]
    if sample:
        outs.append(out(c_conv, F32))
        scratch = []
    else:
        outs.append((jax.ShapeDtypeStruct((n_seq, CONV_HALO, c_conv), F32),
                     pl.BlockSpec((1, CONV_HALO, c_conv), lambda i: (i // tiles_per_seq, 0, 0))))
        scratch = [pltpu.VMEM((tm + CONV_HALO, c_conv), F32), pltpu.VMEM((tm, c_conv), F32)]

    return pl.pallas_call(
        functools.partial(_proj_kernel, tm=tm, tiles_per_seq=tiles_per_seq, sample=sample, d_mix=d_mix),
        grid=(nt,),
        in_specs=specs,
        out_specs=[o[1] for o in outs],
        out_shape=[o[0] for o in outs],
        scratch_shapes=scratch,
        compiler_params=_cparams(("arbitrary",)),
        name=name,
    )(*args)


def _key_to_f32(key):
    bits = key ^ ((key >> 31) & jnp.int32(0x7FFFFFFF))
    return lax.bitcast_convert_type(bits, F32)


def _chunk_loop(n, body, carry, unroll=2):
    assert unroll & (unroll - 1) == 0

    def run(first, count, carry):
        for u in range(count):
            carry = body(first + u, carry)
        return carry

    carry = lax.fori_loop(0, n // unroll, lambda i, c: run(i * unroll, unroll, c), carry)
    done = n // unroll * unroll
    part = unroll // 2
    while part >= 1:
        carry = lax.cond((n - done) // part % 2 == 1, lambda c, d=done, p=part: run(d, p, c), lambda c: c, carry)
        done = done + jnp.where((n - done) // part % 2 == 1, part, 0)
        part //= 2
    return carry


def _count(sc_ref, nchunks, ck, pred):
    rows = sc_ref.shape[0]

    def body(c, acc):
        start = pl.multiple_of(c * ck, ck)
        hit = jnp.where(pred(sc_ref[:, pl.ds(start, ck)], start), 1.0, 0.0)
        part = hit[:, 0:LANES]
        for i in range(1, ck // LANES):
            part = part + hit[:, i * LANES:(i + 1) * LANES]
        return acc + part

    acc = lax.fori_loop(0, nchunks, body, jnp.zeros((rows, LANES), F32))
    return jnp.sum(acc, axis=1, keepdims=True)


def _exact_threshold(sc_ref, nchunks, ck, n_sel, n_adm, total_cols):
    rows = sc_ref.shape[0]
    want = jnp.float32(n_sel)

    def bisect(i, carry):
        t, ct = carry
        cand = t + lax.shift_left(jnp.int32(1), 31 - i)
        thr = _key_to_f32(cand)
        cnt = _count(sc_ref, nchunks, ck, lambda blk, _: blk >= thr)
        take = cnt >= want
        return jnp.where(take, cand, t), jnp.where(take, cnt, ct)

    t, ct = lax.fori_loop(0, 32, bisect, (jnp.full((rows, 1), INT_MIN, I32), jnp.zeros((rows, 1), F32)))
    full = n_adm <= n_sel
    thr = jnp.where(full, F32_LOWEST, _key_to_f32(t))
    tied = jnp.logical_and(jnp.logical_not(full), ct > want)

    @pl.when(jnp.sum(jnp.where(tied, 1.0, 0.0)) > 0.0)
    def _():
        n_gt = _count(sc_ref, nchunks, ck, lambda blk, _: blk > thr)
        room = want - n_gt

        def col_index(shape, start):
            return start + lax.broadcasted_iota(I32, shape, 1)

        nbits = max(int(total_cols - 1).bit_length(), 1)

        def search(i, q):
            cand = q + lax.shift_left(jnp.int32(1), nbits - 1 - i)
            below = _count(sc_ref, nchunks, ck,
                           lambda blk, s: jnp.logical_and(blk == thr, col_index(blk.shape, s) < cand))
            return jnp.where(below < room, cand, q)

        last = lax.fori_loop(0, nbits, search, jnp.zeros((rows, 1), I32))

        def drop(c, _):
            start = pl.multiple_of(c * ck, ck)
            blk = sc_ref[:, pl.ds(start, ck)]
            lose = jnp.logical_and(tied, jnp.logical_and(blk == thr, col_index(blk.shape, start) > last))
            sc_ref[:, pl.ds(start, ck)] = jnp.where(lose, -jnp.inf, blk)
            return 0

        lax.fori_loop(0, nchunks, drop, 0)

    return thr


def _select_threshold(sc_ref, thr_ref, nchunks, ck, n_sel, n_adm, total_cols, row_lo, row_hi):
    rows = sc_ref.shape[0]
    want = jnp.float32(n_sel)
    full = n_adm <= n_sel

    def unresolved(done):
        return jnp.sum(done) < rows

    def cond(state):
        it, go = state[0], state[1]
        return jnp.logical_and(it < SEARCH_CAP, go)

    def body(state):
        it, _, lo, hi, thr, done = state
        mid = lo + (hi - lo) * 0.5
        cnt = _count(sc_ref, nchunks, ck, lambda blk, _: blk >= mid)
        go = unresolved(done)
        found = cnt == want
        thr = jnp.where(jnp.logical_and(found, done == 0.0), mid, thr)
        done = jnp.where(found, 1.0, done)
        return it + 1, go, jnp.where(cnt > want, mid, lo), jnp.where(cnt < want, mid, hi), thr, done

    done0 = jnp.where(full, 1.0, 0.0)
    init = (jnp.int32(0), unresolved(done0), row_lo, row_hi, jnp.where(full, F32_LOWEST, row_lo), done0)
    _, _, _, _, thr, done = lax.while_loop(cond, body, init)
    thr_ref[...] = thr

    @pl.when(unresolved(done))
    def _():
        exact = _exact_threshold(sc_ref, nchunks, ck, n_sel, n_adm, total_cols)
        thr_ref[...] = jnp.where(done > 0.0, thr, exact)

    return thr_ref[...]


def _attn_kernel(qi_ref, w_ref, kia_ref, kib_ref, q_ref, k_ref, v_ref, o_ref, sc_ref, thr_ref, s_ref,
                 *, tq, th, n_sel, seq_len):
    t0 = pl.program_id(1) * tq
    nk = t0 // ATT_CK + 1
    row_pos = t0 + lax.broadcasted_iota(I32, (tq, 1), 0)
    n_heads = q_ref.shape[2] // LANES
    group = n_heads // N_KV_HEADS

    def fold(op, acc, x):
        for i in range(x.shape[1] // LANES):
            acc = op(acc, x[:, i * LANES:(i + 1) * LANES])
        return acc

    def score(c, carry):
        hi_run, lo_run = carry
        start = pl.multiple_of(c * ATT_CK_SCORE, ATT_CK_SCORE)
        ka = kia_ref[0, pl.ds(start, ATT_CK_SCORE), :]
        kb = kib_ref[0, pl.ds(start, ATT_CK_SCORE), :]
        acc = jnp.zeros((tq, ATT_CK_SCORE), F32)
        for j in range(H_IDX // 2):
            pair = qi_ref[0, :, j * LANES:(j + 1) * LANES]
            acc = acc + jnp.maximum(_dot_nt(pair, ka), 0.0) * w_ref[0, :, 2 * j:2 * j + 1]
            acc = acc + jnp.maximum(_dot_nt(pair, kb), 0.0) * w_ref[0, :, 2 * j + 1:2 * j + 2]
        admissible = start + lax.broadcasted_iota(I32, (tq, ATT_CK_SCORE), 1) <= row_pos
        masked = jnp.where(admissible, acc, -jnp.inf)
        sc_ref[:, pl.ds(start, ATT_CK_SCORE)] = masked
        return fold(jnp.maximum, hi_run, masked), fold(jnp.minimum, lo_run, jnp.where(admissible, acc, jnp.inf))

    hi_run, lo_run = _chunk_loop(nk * (ATT_CK // ATT_CK_SCORE), score,
                                 (jnp.full((tq, LANES), -jnp.inf, F32), jnp.full((tq, LANES), jnp.inf, F32)))

    thr = _select_threshold(sc_ref, thr_ref, nk, ATT_CK, n_sel, row_pos + 1, seq_len,
                            jnp.min(lo_run, axis=1, keepdims=True), jnp.max(hi_run, axis=1, keepdims=True))

    def to_mask(c, _):
        start = pl.multiple_of(c * ATT_CK, ATT_CK)
        sc_ref[:, pl.ds(start, ATT_CK)] = jnp.where(sc_ref[:, pl.ds(start, ATT_CK)] >= thr, 0.0, NEG_BIG)
        return 0

    lax.fori_loop(0, nk, to_mask, 0)

    rows = group * th

    def attend(u, _):
        r0 = pl.multiple_of(u // N_KV_HEADS * th, th)
        n = u % N_KV_HEADS
        kv_lanes = pl.ds(pl.multiple_of(n // 2 * LANES, LANES), LANES)
        qg = jnp.concatenate(
            [q_ref[0, pl.ds(r0, th), pl.ds(pl.multiple_of((group * n + g) * LANES, LANES), LANES)]
             for g in range(group)], axis=0)

        def logits(c, m_run):
            start = pl.multiple_of(c * ATT_CK, ATT_CK)
            s = (_dot_nt(qg, k_ref[0, pl.ds(start, ATT_CK), kv_lanes])
                 + jnp.concatenate([sc_ref[pl.ds(r0, th), pl.ds(start, ATT_CK)]] * group, axis=0))
            s_ref[:, pl.ds(start, ATT_CK)] = s
            return fold(jnp.maximum, m_run, s)

        m_run = _chunk_loop(nk, logits, jnp.full((rows, LANES), -jnp.inf, F32), unroll=ATT_UNROLL)
        m = jnp.max(m_run, axis=1, keepdims=True)

        def weigh(c, carry):
            l_run, acc = carry
            start = pl.multiple_of(c * ATT_CK, ATT_CK)
            p = jnp.exp(s_ref[:, pl.ds(start, ATT_CK)] - m)
            acc = acc + _dot(p.astype(BF16), v_ref[0, pl.ds(start, ATT_CK), kv_lanes])
            return fold(jnp.add, l_run, p), acc

        l_run, acc = _chunk_loop(nk, weigh, (jnp.zeros((rows, LANES), F32), jnp.zeros((rows, LANES), F32)),
                                 unroll=ATT_UNROLL)
        out = acc / jnp.sum(l_run, axis=1, keepdims=True)
        out = jnp.where(n % 2 == 1, out[:, HEAD_DIM:], out[:, :HEAD_DIM])
        width = group * HEAD_DIM
        o_ref[0, pl.ds(r0, th), pl.ds(pl.multiple_of(n * width, width), width)] = jnp.concatenate(
            [out[g * th:(g + 1) * th] for g in range(group)], axis=1).astype(o_ref.dtype)
        return 0

    lax.fori_loop(0, tq // th * N_KV_HEADS, attend, 0)


def _prompt_attention(qi, wi, kia, kib, qpad, kb, vb, *, n_sel):
    b, s, _ = qi.shape
    tq = min(ATT_TQ, s)
    th = min(ATT_TH, tq)
    assert s % ATT_CK == 0 and ATT_CK % tq == 0 and ATT_CK % ATT_CK_SCORE == 0 and tq % th == 0
    n_heads = qpad.shape[2] // LANES
    tile = lambda bi, i: (bi, i, 0)
    seq = lambda bi, i: (bi, 0, 0)
    return pl.pallas_call(
        functools.partial(_attn_kernel, tq=tq, th=th, n_sel=n_sel, seq_len=s),
        grid=(b, s // tq),
        in_specs=[pl.BlockSpec((1, tq, qi.shape[2]), tile), pl.BlockSpec((1, tq, wi.shape[2]), tile),
                  _resident((1, s, LANES), seq), _resident((1, s, LANES), seq),
                  pl.BlockSpec((1, tq, qpad.shape[2]), tile),
                  _resident((1, s, kb.shape[2]), seq), _resident((1, s, vb.shape[2]), seq)],
        out_specs=pl.BlockSpec((1, tq, n_heads * HEAD_DIM), tile),
        out_shape=jax.ShapeDtypeStruct((b, s, n_heads * HEAD_DIM), BF16),
        scratch_shapes=[pltpu.VMEM((tq, s), F32), pltpu.VMEM((tq, 1), F32),
                        pltpu.VMEM((n_heads // N_KV_HEADS * th, s), F32)],
        compiler_params=_cparams(("arbitrary", "arbitrary")),
        name="prompt_attention",
    )(qi, wi, kia, kib, qpad, kb, vb)


def _sample_select_kernel(pt_ref, qi_ref, w_ref, kin_ref, cache_ref, idx_ref, buf_ref, row_ref, sc_ref, thr_ref, sem,
                          *, layer, n_pages, page, n_sel, rows):
    b = pl.program_id(0)
    past = n_pages * page
    total = rows * LANES

    def page_copy(j):
        return pltpu.make_async_copy(cache_ref.at[layer, pt_ref[b, j]], buf_ref.at[pl.ds(j * page, page)], sem)

    def start(j, _):
        page_copy(j).start()
        return 0

    def wait(j, _):
        page_copy(j).wait()
        return 0

    lax.fori_loop(0, n_pages, start, 0)

    @pl.when(b == 0)
    def _():
        buf_ref[pl.ds(past, total - past), :] = jnp.zeros((total - past, D_IDX), F32)

    lax.fori_loop(0, n_pages, wait, 0)
    buf_ref[pl.ds(past, 1), :] = kin_ref[0]

    x = _dot_nt(qi_ref[0], buf_ref[...].astype(BF16))
    score = jnp.sum(jnp.maximum(x, 0.0) * w_ref[0], axis=0, keepdims=True)
    admissible = lax.broadcasted_iota(I32, (1, total), 1) <= past
    masked = jnp.where(admissible, score, -jnp.inf)
    row_ref[...] = masked
    thr = _select_threshold(row_ref, thr_ref, total // ATT_CK, ATT_CK, n_sel, jnp.full((1, 1), past + 1, I32), total,
                            jnp.min(jnp.where(admissible, score, jnp.inf), axis=1, keepdims=True),
                            jnp.max(masked, axis=1, keepdims=True))
    for r in range(rows):
        sc_ref[r:r + 1, :] = row_ref[:, r * LANES:(r + 1) * LANES]
    keep = sc_ref[...] >= thr

    kb16 = jnp.where(keep, 1.0, 0.0).astype(BF16)
    li = lax.broadcasted_iota(I32, (LANES, LANES), 0)
    lj = lax.broadcasted_iota(I32, (LANES, LANES), 1)
    within = _dot(kb16, jnp.where(li <= lj, 1.0, 0.0).astype(BF16))
    rowsum = within[:, LANES - 1:LANES]
    ri = lax.broadcasted_iota(I32, (rows, rows), 0)
    rj = lax.broadcasted_iota(I32, (rows, rows), 1)
    incl = _dot(jnp.where(rj <= ri, 1.0, 0.0).astype(BF16),
                jnp.broadcast_to(rowsum, (rows, LANES)).astype(BF16))[:, 0:1]
    slot = lax.broadcasted_iota(I32, (rows, n_sel), 1).astype(F32)
    row_of = jnp.sum(jnp.where(incl <= slot, 1.0, 0.0), axis=0, keepdims=True)
    onehot = lax.broadcasted_iota(I32, (rows, n_sel), 0).astype(F32) == row_of
    row_off = jnp.sum(jnp.where(onehot, incl - rowsum, 0.0), axis=0, keepdims=True)
    rank = slot[0:1, :] - row_off + 1.0
    marked = jnp.where(keep, within, 0.0).astype(BF16)
    ranks_t = lax.dot_general(marked, jnp.where(onehot, 1.0, 0.0).astype(BF16),
                              (((0,), (0,)), ((), ())), preferred_element_type=F32)
    lane_id = lax.broadcasted_iota(I32, (LANES, n_sel), 0).astype(F32)
    lane_of = jnp.sum(jnp.where(ranks_t == rank, lane_id, 0.0), axis=0, keepdims=True)
    idx_ref[0] = (row_of * LANES + lane_of).astype(I32)


def _sample_select(page_table, qi, wi, ki_new, cache_idx, *, layer, n_sel):
    db, n_pages = page_table.shape
    page = cache_idx.shape[2]
    past = n_pages * page
    total = _round_up(past + 1, SUBLANES * LANES)
    assert total % ATT_CK == 0
    rows = total // LANES
    grid_spec = pltpu.PrefetchScalarGridSpec(
        num_scalar_prefetch=1,
        grid=(db,),
        in_specs=[pl.BlockSpec((1, H_IDX, D_IDX), lambda b, pt: (b, 0, 0)),
                  pl.BlockSpec((1, H_IDX, 1), lambda b, pt: (b, 0, 0)),
                  pl.BlockSpec((1, 1, D_IDX), lambda b, pt: (b, 0, 0)),
                  pl.BlockSpec(memory_space=pl.ANY)],
        out_specs=pl.BlockSpec((1, 1, n_sel), lambda b, pt: (b, 0, 0)),
        scratch_shapes=[pltpu.VMEM((total, D_IDX), F32), pltpu.VMEM((1, total), F32),
                        pltpu.VMEM((rows, LANES), F32), pltpu.VMEM((1, 1), F32), pltpu.SemaphoreType.DMA(())],
    )
    return pl.pallas_call(
        functools.partial(_sample_select_kernel, layer=layer, n_pages=n_pages, page=page, n_sel=n_sel, rows=rows),
        grid_spec=grid_spec,
        out_shape=jax.ShapeDtypeStruct((db, 1, n_sel), I32),
        compiler_params=_cparams(("arbitrary",)),
        name="sample_select",
    )(page_table, qi, wi, ki_new, cache_idx)


def _sample_attend_kernel(idx_ref, pt_ref, q_ref, knew_ref, vnew_ref, ck_ref, cv_ref, o_ref, kbuf, vbuf, sem,
                          *, layer, n_pages, page, n_sel):
    b = pl.program_id(0)
    past = n_pages * page
    n_heads = q_ref.shape[1]
    group = n_heads // N_KV_HEADS

    def dst(buf, j):
        return buf.at[pl.ds(j * N_KV_HEADS, N_KV_HEADS)]

    def cached_rows(j, i):
        p = jnp.minimum(i, past - 1)
        phys = pt_ref[b, p // page]
        slot = p % page
        return (pltpu.make_async_copy(ck_ref.at[layer, phys, slot], dst(kbuf, j), sem),
                pltpu.make_async_copy(cv_ref.at[layer, phys, slot], dst(vbuf, j), sem))

    def start(j, _):
        i = idx_ref[b, j]

        @pl.when(i < past)
        def _():
            for cp in cached_rows(j, i):
                cp.start()

        @pl.when(i >= past)
        def _():
            pltpu.make_async_copy(knew_ref.at[b], dst(kbuf, j), sem).start()
            pltpu.make_async_copy(vnew_ref.at[b], dst(vbuf, j), sem).start()
        return 0

    def wait(j, _):
        for cp in cached_rows(j, idx_ref[b, j]):
            cp.wait()
        return 0

    lax.fori_loop(0, n_sel, start, 0)
    lax.fori_loop(0, n_sel, wait, 0)

    q = q_ref[0]
    kv_of_row = lax.broadcasted_iota(I32, (n_heads, 1), 0) // group
    s = jnp.zeros((n_heads, n_sel), F32)
    for n in range(N_KV_HEADS):
        k_n = kbuf[pl.ds(n, n_sel, stride=N_KV_HEADS), :].astype(BF16)
        s = jnp.where(kv_of_row == n, _dot_nt(q, k_n), s)
    p = jnp.exp(s - jnp.max(s, axis=1, keepdims=True))
    p = (p / jnp.sum(p, axis=1, keepdims=True)).astype(BF16)
    o = jnp.zeros((n_heads, HEAD_DIM), F32)
    for n in range(N_KV_HEADS):
        v_n = vbuf[pl.ds(n, n_sel, stride=N_KV_HEADS), :].astype(BF16)
        o = jnp.where(kv_of_row == n, _dot(p, v_n), o)
    o_ref[0] = o


def _sample_attend(idx, page_table, q, k_new, v_new, cache_k, cache_v, *, layer, n_sel):
    db, n_pages = page_table.shape
    page = cache_k.shape[2]
    n_heads = q.shape[1]
    grid_spec = pltpu.PrefetchScalarGridSpec(
        num_scalar_prefetch=2,
        grid=(db,),
        in_specs=[pl.BlockSpec((1, n_heads, HEAD_DIM), lambda b, ix, pt: (b, 0, 0)),
                  pl.BlockSpec(memory_space=pl.ANY), pl.BlockSpec(memory_space=pl.ANY),
                  pl.BlockSpec(memory_space=pl.ANY), pl.BlockSpec(memory_space=pl.ANY)],
        out_specs=pl.BlockSpec((1, n_heads, HEAD_DIM), lambda b, ix, pt: (b, 0, 0)),
        scratch_shapes=[pltpu.VMEM((n_sel * N_KV_HEADS, HEAD_DIM), F32),
                        pltpu.VMEM((n_sel * N_KV_HEADS, HEAD_DIM), F32),
                        pltpu.SemaphoreType.DMA(())],
    )
    return pl.pallas_call(
        functools.partial(_sample_attend_kernel, layer=layer, n_pages=n_pages, page=page, n_sel=n_sel),
        grid_spec=grid_spec,
        out_shape=jax.ShapeDtypeStruct((db, n_heads, HEAD_DIM), F32),
        compiler_params=_cparams(("arbitrary",)),
        name="sample_attend",
    )(idx, page_table, q, k_new, v_new, cache_k, cache_v)


def _pack_ffn(w_in, w_out):
    d, f2 = w_in.shape
    f = f2 // 2
    fp = _round_up(f, FFN_TF)
    pad = lambda w, axis: jnp.pad(w, [(0, fp - f) if a == axis else (0, 0) for a in range(2)])
    return (pad(w_in[:, :f], 1).astype(BF16), pad(w_in[:, f:], 1).astype(BF16), pad(w_out, 0).astype(BF16))


def _pack_proj(w_in, d_mix):
    d = w_in.shape[0]
    c_conv = d_mix // C_CONV_FRACTION
    d_attn = d_mix - c_conv
    n_heads = d_attn // HEAD_DIM
    kvw = N_KV_HEADS * HEAD_DIM
    sizes = [2 * c_conv, d_attn, kvw, kvw, H_IDX * D_IDX, D_IDX, H_IDX]
    parts, o = [], 0
    for sz in sizes:
        parts.append(w_in[:, o:o + sz])
        o += sz
    p_conv, q, k, v, qi, ki, wi = parts
    qh = q.reshape(d, n_heads, HEAD_DIM)
    zero = jnp.zeros_like(qh)
    odd = ((jnp.arange(n_heads) // (n_heads // N_KV_HEADS)) % 2 == 1)[None, :, None]
    qpad = jnp.concatenate([jnp.where(odd, zero, qh), jnp.where(odd, qh, zero)], axis=-1).reshape(d, n_heads * LANES)
    z = jnp.zeros((d, LANES - D_IDX), w_in.dtype)
    cols = [p_conv[:, :c_conv], p_conv[:, c_conv:], qpad, k, v, qi,
            jnp.concatenate([ki, z], axis=1), jnp.concatenate([z, ki], axis=1),
            jnp.concatenate([wi, jnp.zeros((d, LANES - H_IDX), w_in.dtype)], axis=1)]
    w_all = jnp.concatenate(cols, axis=1).astype(BF16)
    assert w_all.shape[1] == _proj_layout(d_mix)[1]
    return w_all


def _own_half(o_pad, n_heads):
    odd = ((jnp.arange(n_heads) // (n_heads // N_KV_HEADS)) % 2 == 1)[:, None]
    return jnp.where(odd, o_pad[..., HEAD_DIM:], o_pad[..., :HEAD_DIM])


def kernel(x_prompt, x_sample, cache_k, cache_v, cache_idx_k, state_conv, page_table, norm_ffn1, ffn1_w_in, ffn1_w_out, norm_mix, w_in, conv_w, conv_b, conv_ln_g, conv_ln_b, w_out, norm_ffn2, ffn2_w_in, ffn2_w_out, norm_final):
    depth = w_in.shape[0]
    bsz, seq, d = x_prompt.shape
    db, dseq, _ = x_sample.shape
    assert dseq == 1
    d_mix = d
    c_conv = d_mix // C_CONV_FRACTION
    n_heads = (d_mix - c_conv) // HEAD_DIM
    page = cache_k.shape[2]
    n_pages = page_table.shape[1]
    past = n_pages * page
    n_sel_p = min(TOPK_MAX, seq // 4)
    n_sel_s = min(TOPK_MAX, (past + dseq) // 4)

    xp = x_prompt.reshape(bsz * seq, d)
    xs = x_sample.reshape(db, d)
    g_final = norm_final.reshape(1, d)
    outs_p, outs_s = [], []

    for l in range(depth):
        f1 = _pack_ffn(ffn1_w_in[l], ffn1_w_out[l])
        f2 = _pack_ffn(ffn2_w_in[l], ffn2_w_out[l])
        w_all = _pack_proj(w_in[l], d_mix)
        conv_p = jnp.stack([conv_b[l], conv_ln_g[l], conv_ln_b[l]])
        wo = w_out[l].astype(BF16)
        woc, woa = wo[:c_conv], wo[c_conv:]
        g1, gm, g2 = norm_ffn1[l].reshape(1, d), norm_mix[l].reshape(1, d), norm_ffn2[l].reshape(1, d)
        last = g_final if l == depth - 1 else None

        xp = _ffn(xp, g1, *f1, name="ffn1_prompt")
        (conv_o, qpad, k_new, v_new, kb, vb, qi, kia, kib, ki_new, wi, u_tail) = _proj(
            xp, gm, w_all, conv_w[l], conv_p, seq_len=seq, name="proj_prompt")
        r3 = lambda a: a.reshape(bsz, seq, a.shape[-1])
        attn = _prompt_attention(r3(qi), r3(wi), r3(kia), r3(kib), r3(qpad), r3(kb), r3(vb), n_sel=n_sel_p)
        xp = _ffn(xp, g2, *f2, mix=(conv_o, attn.reshape(bsz * seq, -1), woc, woa), g_final=last,
                  name="ffn2_prompt")
        outs_p.append((k_new.reshape(bsz, seq, N_KV_HEADS, HEAD_DIM), v_new.reshape(bsz, seq, N_KV_HEADS, HEAD_DIM),
                       ki_new.reshape(bsz, seq, D_IDX), u_tail[:, CONV_HALO - (CONV_WIDTH - 1):, :]))

        state = state_conv[l].astype(F32)
        xs = _ffn(xs, g1, *f1, name="ffn1_sample")
        (conv_o, qpad, k_new, v_new, _, _, qi, _, _, ki_new, wi, u_new) = _proj(
            xs, gm, w_all, conv_w[l], conv_p, seq_len=1, state=jnp.swapaxes(state, 0, 1), name="proj_sample")
        idx = _sample_select(page_table, qi.reshape(db, H_IDX, D_IDX), wi.reshape(db, H_IDX, 1),
                             ki_new.reshape(db, 1, D_IDX), cache_idx_k, layer=l, n_sel=n_sel_s)
        attn = _sample_attend(idx.reshape(db, n_sel_s), page_table,
                              _own_half(qpad.reshape(db, n_heads, LANES), n_heads),
                              k_new.reshape(db, N_KV_HEADS, HEAD_DIM), v_new.reshape(db, N_KV_HEADS, HEAD_DIM),
                              cache_k, cache_v, layer=l, n_sel=n_sel_s)
        attn = attn.reshape(db, n_heads * HEAD_DIM).astype(BF16)
        xs = _ffn(xs, g2, *f2, mix=(conv_o, attn, woc, woa), g_final=last, name="ffn2_sample")
        outs_s.append((k_new.reshape(db, 1, N_KV_HEADS, HEAD_DIM), v_new.reshape(db, 1, N_KV_HEADS, HEAD_DIM),
                       ki_new.reshape(db, 1, D_IDX),
                       jnp.concatenate([state[:, 1:, :], u_new[:, None, :]], axis=1)))

    stack = lambda outs, i: jnp.stack([o[i] for o in outs])
    return (xp.reshape(bsz, seq, d), xs.reshape(db, 1, d),
            stack(outs_p, 0), stack(outs_p, 1), stack(outs_p, 2), stack(outs_p, 3),
            stack(outs_s, 0), stack(outs_s, 1), stack(outs_s, 2), stack(outs_s, 3))
```

```python
import functools

import jax
import jax.numpy as jnp
from jax import lax
from jax.experimental import pallas as pl
from jax.experimental.pallas import tpu as pltpu

F32 = jnp.float32
BF16 = jnp.bfloat16
I32 = jnp.int32

C_CONV_FRACTION = 2
CONV_WIDTH = 31
HEAD_DIM = 64
N_KV_HEADS = 4
H_IDX = 16
D_IDX = 64
TOPK_MAX = 256
EPS = 1e-6
ATTN_SCALE = HEAD_DIM ** -0.5
IDX_SCALE = (D_IDX ** -0.5) * (H_IDX ** -0.5)

LANES = 128
SUBLANES = 8
VMEM_LIMIT_BYTES = 56 * 1024 * 1024

FFN_TM = 512
FFN_TF = 512
PROJ_TM = 256
CONV_HALO = 32
ATT_TQ = 128
ATT_TH = 128
ATT_CK_SCORE = 256
ATT_CK = 512
ATT_UNROLL = 4
SEARCH_CAP = 40

INT_MIN = -2 ** 31
NEG_BIG = -1e30
F32_LOWEST = float(jnp.finfo(jnp.float32).min)


def _round_up(x, m):
    return (x + m - 1) // m * m


def _cparams(sem):
    return pltpu.CompilerParams(dimension_semantics=sem, vmem_limit_bytes=VMEM_LIMIT_BYTES)


def _resident(shape, index_map):
    return pl.BlockSpec(shape, index_map, pipeline_mode=pl.Buffered(1))


def _rms(x, g):
    ms = jnp.mean(x * x, axis=-1, keepdims=True)
    return x * lax.rsqrt(ms + EPS) * g


def _dot(a, b):
    return jnp.dot(a, b, preferred_element_type=F32)


def _dot_nt(a, b):
    return lax.dot_general(a, b, (((1,), (1,)), ((), ())), preferred_element_type=F32)


def _ffn_kernel(*refs, has_mix, has_final):
    it = iter(refs)
    x_ref = next(it)
    if has_mix:
        mc_ref, ma_ref, woc_ref, woa_ref = next(it), next(it), next(it), next(it)
    g_ref, wa_ref, wb_ref, wo_ref = next(it), next(it), next(it), next(it)
    if has_final:
        gf_ref = next(it)
    o_ref = next(it)
    xn_ref = next(it)

    f = pl.program_id(1)

    @pl.when(f == 0)
    def _():
        x = x_ref[...]
        if has_mix:
            x = x + _dot(mc_ref[...], woc_ref[...]) + _dot(ma_ref[...], woa_ref[...])
        o_ref[...] = x
        xn_ref[...] = _rms(x, g_ref[...]).astype(BF16)

    xn = xn_ref[...]
    a = _dot(xn, wa_ref[...])
    b = _dot(xn, wb_ref[...])
    act = (a * jax.nn.sigmoid(a) * b).astype(BF16)
    o_ref[...] += 0.5 * _dot(act, wo_ref[...])

    if has_final:
        @pl.when(f == pl.num_programs(1) - 1)
        def _():
            o_ref[...] = _rms(o_ref[...], gf_ref[...])


def _ffn(x, g, wa, wb, wo, *, mix=None, g_final=None, name):
    m, d = x.shape
    fp = wa.shape[1]
    tm = min(FFN_TM, m)
    assert m % tm == 0 and fp % FFN_TF == 0
    nf = fp // FFN_TF
    has_mix = mix is not None
    has_final = g_final is not None

    row = lambda i, f: (i, 0)
    const = lambda i, f: (0, 0)
    args = [x]
    specs = [pl.BlockSpec((tm, d), row)]
    if has_mix:
        mc, ma, woc, woa = mix
        args += [mc, ma, woc, woa]
        specs += [pl.BlockSpec((tm, mc.shape[1]), row), pl.BlockSpec((tm, ma.shape[1]), row),
                  _resident(woc.shape, const), _resident(woa.shape, const)]
    args += [g, wa, wb, wo]
    specs += [_resident((1, d), const),
              pl.BlockSpec((d, FFN_TF), lambda i, f: (0, f)),
              pl.BlockSpec((d, FFN_TF), lambda i, f: (0, f)),
              pl.BlockSpec((FFN_TF, d), lambda i, f: (f, 0))]
    if has_final:
        args.append(g_final)
        specs.append(_resident((1, d), const))

    return pl.pallas_call(
        functools.partial(_ffn_kernel, has_mix=has_mix, has_final=has_final),
        grid=(m // tm, nf),
        in_specs=specs,
        out_specs=pl.BlockSpec((tm, d), row),
        out_shape=jax.ShapeDtypeStruct((m, d), F32),
        scratch_shapes=[pltpu.VMEM((tm, d), BF16)],
        compiler_params=_cparams(("arbitrary", "arbitrary")),
        name=name,
    )(*args)


def _proj_layout(d_mix):
    c_conv = d_mix // C_CONV_FRACTION
    d_attn = d_mix - c_conv
    n_heads = d_attn // HEAD_DIM
    widths = dict(ca=c_conv, cg=c_conv, qpad=n_heads * LANES, k=N_KV_HEADS * HEAD_DIM,
                  v=N_KV_HEADS * HEAD_DIM, qi=H_IDX * D_IDX, kia=LANES, kib=LANES, wi=LANES)
    off, o = {}, 0
    for name, w in widths.items():
        off[name] = (o, o + w)
        o += w
    return off, o


def _proj_kernel(*refs, tm, tiles_per_seq, sample, d_mix):
    it = iter(refs)
    x_ref, g_ref, w_ref, cw_ref, cp_ref = next(it), next(it), next(it), next(it), next(it)
    if sample:
        st_ref = next(it)
    (conv_ref, qpad_ref, k_ref, v_ref, kb_ref, vb_ref, qi_ref, kia_ref, kib_ref, ki_ref, wi_ref,
     u_ref) = (next(it) for _ in range(12))
    if not sample:
        win_ref, y_ref, z_ref = next(it), next(it), next(it)

    off, _ = _proj_layout(d_mix)
    c_conv = d_mix // C_CONV_FRACTION

    def cols(name):
        lo, hi = off[name]
        return w_ref[:, lo:hi]

    if not sample:
        @pl.when(pl.program_id(0) % tiles_per_seq == 0)
        def _():
            win_ref[0:CONV_HALO, :] = jnp.zeros((CONV_HALO, c_conv), F32)

    xn = _rms(x_ref[...], g_ref[...]).astype(BF16)
    u = _dot(xn, cols("ca")) * jax.nn.sigmoid(_dot(xn, cols("cg")))

    qpad_ref[...] = (_dot(xn, cols("qpad")) * ATTN_SCALE).astype(BF16)
    kk = _dot(xn, cols("k"))
    vv = _dot(xn, cols("v"))
    k_ref[...] = kk
    v_ref[...] = vv
    kb_ref[...] = kk.astype(BF16)
    vb_ref[...] = vv.astype(BF16)
    qi_ref[...] = _dot(xn, cols("qi")).astype(BF16)
    kia = _dot(xn, cols("kia"))
    kia_ref[...] = kia.astype(BF16)
    kib_ref[...] = _dot(xn, cols("kib")).astype(BF16)
    ki_ref[...] = kia[:, :D_IDX]
    wi_ref[...] = _dot(xn, cols("wi"))[:, :H_IDX] * IDX_SCALE

    bias = cp_ref[0:1, :]
    ln_g = cp_ref[1:2, :]
    ln_b = cp_ref[2:3, :]

    if sample:
        u_ref[...] = u
        y = bias + cw_ref[CONV_WIDTH - 1:CONV_WIDTH, :] * u
        for j in range(CONV_WIDTH - 1):
            y = y + cw_ref[j:j + 1, :] * st_ref[j]
    else:
        win_ref[CONV_HALO:CONV_HALO + tm, :] = u
        first = CONV_HALO - (CONV_WIDTH - 1)
        for c in range(c_conv // LANES):
            cs = slice(c * LANES, (c + 1) * LANES)
            acc = jnp.zeros((tm, LANES), F32) + bias[:, cs]
            for r in range(SUBLANES):
                taps = [j for j in range(CONV_WIDTH) if (first + j) % SUBLANES == r]
                if taps:
                    base = first + taps[0]
                    span = first + taps[-1] + tm - base
                    if r:
                        z_ref[0:span, :] = win_ref[base:base + span, cs]
                    for j in taps:
                        lo = first + j - base
                        z = z_ref[lo:lo + tm, :] if r else win_ref[first + j:first + j + tm, cs]
                        acc = acc + cw_ref[j:j + 1, cs] * z
            y_ref[:, cs] = acc
        y = y_ref[...]
        tail = win_ref[tm:tm + CONV_HALO, :]
        u_ref[0] = tail
        win_ref[0:CONV_HALO, :] = tail

    mu = jnp.mean(y, axis=-1, keepdims=True)
    var = jnp.mean(jnp.square(y - mu), axis=-1, keepdims=True)
    yn = (y - mu) * lax.rsqrt(var + EPS) * ln_g + ln_b
    conv_ref[...] = (yn * jax.nn.sigmoid(yn)).astype(BF16)


def _proj(x, g, w_all, conv_w, conv_p, *, seq_len, state=None, name):
    m, d = x.shape
    d_mix = d
    c_conv = d_mix // C_CONV_FRACTION
    n_heads = (d_mix - c_conv) // HEAD_DIM
    kvw = N_KV_HEADS * HEAD_DIM
    sample = state is not None
    tm = m if sample else min(PROJ_TM, seq_len)
    assert m % tm == 0 and seq_len % tm == 0 or sample
    assert tm >= CONV_HALO or sample
    nt = m // tm
    tiles_per_seq = max(seq_len // tm, 1)
    n_seq = m // seq_len

    row = lambda i: (i, 0)
    const = lambda i: (0, 0)
    args = [x, g, w_all, conv_w, conv_p]
    specs = [pl.BlockSpec((tm, d), row), _resident((1, d), const), _resident(w_all.shape, const),
             _resident(conv_w.shape, const), _resident(conv_p.shape, const)]
    if sample:
        args.append(state)
        specs.append(_resident(state.shape, lambda i: (0, 0, 0)))

    def out(width, dtype):
        return jax.ShapeDtypeStruct((m, width), dtype), pl.BlockSpec((tm, width), row)

    outs = [out(c_conv, BF16), out(n_heads * LANES, BF16), out(kvw, F32), out(kvw, F32),
            out(kvw, BF16), out(kvw, BF16), out(H_IDX * D_IDX, BF16), out(LANES, BF16),
            out(LANES, BF16), out(D_IDX, F32), out(H_IDX, F32)]
    if sample:
        outs.append(out(c_conv, F32))
        scratch = []
    else:
        outs.append((jax.ShapeDtypeStruct((n_seq, CONV_HALO, c_conv), F32),
                     pl.BlockSpec((1, CONV_HALO, c_conv), lambda i: (i // tiles_per_seq, 0, 0))))
        scratch = [pltpu.VMEM((tm + CONV_HALO, c_conv), F32), pltpu.VMEM((tm, c_conv), F32),
                   pltpu.VMEM((tm + CONV_HALO, LANES), F32)]

    return pl.pallas_call(
        functools.partial(_proj_kernel, tm=tm, tiles_per_seq=tiles_per_seq, sample=sample, d_mix=d_mix),
        grid=(nt,),
        in_specs=specs,
        out_specs=[o[1] for o in outs],
        out_shape=[o[0] for o in outs],
        scratch_shapes=scratch,
        compiler_params=_cparams(("arbitrary",)),
        name=name,
    )(*args)


def _key_to_f32(key):
    bits = key ^ ((key >> 31) & jnp.int32(0x7FFFFFFF))
    return lax.bitcast_convert_type(bits, F32)


def _chunk_loop(n, body, carry, unroll=2):
    assert unroll & (unroll - 1) == 0

    def run(first, count, carry):
        for u in range(count):
            carry = body(first + u, carry)
        return carry

    carry = lax.fori_loop(0, n // unroll, lambda i, c: run(i * unroll, unroll, c), carry)
    done = n // unroll * unroll
    part = unroll // 2
    while part >= 1:
        carry = lax.cond((n - done) // part % 2 == 1, lambda c, d=done, p=part: run(d, p, c), lambda c: c, carry)
        done = done + jnp.where((n - done) // part % 2 == 1, part, 0)
        part //= 2
    return carry


def _count(sc_ref, nchunks, ck, pred, whole=False):
    rows = sc_ref.shape[0]

    def body(c, acc):
        start = pl.multiple_of(c * ck, ck)
        hit = jnp.where(pred(sc_ref[:, pl.ds(start, ck)], start), 1.0, 0.0)
        part = hit[:, 0:LANES]
        for i in range(1, ck // LANES):
            part = part + hit[:, i * LANES:(i + 1) * LANES]
        return acc + part

    acc = lax.fori_loop(0, nchunks, body, jnp.zeros((rows, LANES), F32))
    per_row = jnp.sum(acc, axis=1, keepdims=True)
    return jnp.sum(per_row, axis=0, keepdims=True) if whole else per_row


def _position(shape, start, whole):
    col = start + lax.broadcasted_iota(I32, shape, 1)
    return lax.broadcasted_iota(I32, shape, 0) * shape[1] + col if whole else col


def _exact_threshold(sc_ref, nchunks, ck, n_sel, n_adm, total_cols, whole=False):
    rows = 1 if whole else sc_ref.shape[0]
    want = jnp.float32(n_sel)

    def bisect(i, carry):
        t, ct = carry
        cand = t + lax.shift_left(jnp.int32(1), 31 - i)
        thr = _key_to_f32(cand)
        cnt = _count(sc_ref, nchunks, ck, lambda blk, _: blk >= thr, whole)
        take = cnt >= want
        return jnp.where(take, cand, t), jnp.where(take, cnt, ct)

    t, ct = lax.fori_loop(0, 32, bisect, (jnp.full((rows, 1), INT_MIN, I32), jnp.zeros((rows, 1), F32)))
    full = n_adm <= n_sel
    thr = jnp.where(full, F32_LOWEST, _key_to_f32(t))
    tied = jnp.logical_and(jnp.logical_not(full), ct > want)

    @pl.when(jnp.sum(jnp.where(tied, 1.0, 0.0)) > 0.0)
    def _():
        n_gt = _count(sc_ref, nchunks, ck, lambda blk, _: blk > thr, whole)
        room = want - n_gt
        nbits = max(int(total_cols - 1).bit_length(), 1)

        def search(i, q):
            cand = q + lax.shift_left(jnp.int32(1), nbits - 1 - i)
            below = _count(sc_ref, nchunks, ck,
                           lambda blk, s: jnp.logical_and(blk == thr, _position(blk.shape, s, whole) < cand), whole)
            return jnp.where(below < room, cand, q)

        last = lax.fori_loop(0, nbits, search, jnp.zeros((rows, 1), I32))

        def drop(c, _):
            start = pl.multiple_of(c * ck, ck)
            blk = sc_ref[:, pl.ds(start, ck)]
            lose = jnp.logical_and(tied, jnp.logical_and(blk == thr, _position(blk.shape, start, whole) > last))
            sc_ref[:, pl.ds(start, ck)] = jnp.where(lose, -jnp.inf, blk)
            return 0

        lax.fori_loop(0, nchunks, drop, 0)

    return thr


def _select_threshold(sc_ref, thr_ref, nchunks, ck, n_sel, n_adm, total_cols, row_lo, row_hi, whole=False):
    rows = 1 if whole else sc_ref.shape[0]
    want = jnp.float32(n_sel)
    full = n_adm <= n_sel

    def unresolved(done):
        return jnp.sum(done) < rows

    def cond(state):
        it, go = state[0], state[1]
        return jnp.logical_and(it < SEARCH_CAP, go)

    def body(state):
        it, _, lo, hi, thr, done = state
        mid = lo + (hi - lo) * 0.5
        cnt = _count(sc_ref, nchunks, ck, lambda blk, _: blk >= mid, whole)
        go = unresolved(done)
        found = cnt == want
        thr = jnp.where(jnp.logical_and(found, done == 0.0), mid, thr)
        done = jnp.where(found, 1.0, done)
        return it + 1, go, jnp.where(cnt > want, mid, lo), jnp.where(cnt < want, mid, hi), thr, done

    done0 = jnp.where(full, 1.0, 0.0)
    init = (jnp.int32(0), unresolved(done0), row_lo, row_hi, jnp.where(full, F32_LOWEST, row_lo), done0)
    _, _, _, _, thr, done = lax.while_loop(cond, body, init)
    thr_ref[...] = thr

    @pl.when(unresolved(done))
    def _():
        exact = _exact_threshold(sc_ref, nchunks, ck, n_sel, n_adm, total_cols, whole)
        thr_ref[...] = jnp.where(done > 0.0, thr, exact)

    return thr_ref[...]


def _attn_kernel(qi_ref, w_ref, kia_ref, kib_ref, q_ref, k_ref, v_ref, o_ref, sc_ref, thr_ref, s_ref, wb_ref,
                 *, tq, th, n_sel, seq_len):
    t0 = pl.program_id(1) * tq
    nk = t0 // ATT_CK + 1
    row_pos = t0 + lax.broadcasted_iota(I32, (tq, 1), 0)
    n_heads = q_ref.shape[2] // LANES
    group = n_heads // N_KV_HEADS

    def fold(op, acc, x):
        for i in range(x.shape[1] // LANES):
            acc = op(acc, x[:, i * LANES:(i + 1) * LANES])
        return acc

    for h in range(H_IDX):
        wb_ref[h] = jnp.broadcast_to(w_ref[0, :, h:h + 1], (tq, LANES))

    def head_weight(h):
        return jnp.concatenate([wb_ref[h]] * (ATT_CK_SCORE // LANES), axis=1)

    def score(c, carry):
        hi_run, lo_run = carry
        start = pl.multiple_of(c * ATT_CK_SCORE, ATT_CK_SCORE)
        ka = kia_ref[0, pl.ds(start, ATT_CK_SCORE), :]
        kb = kib_ref[0, pl.ds(start, ATT_CK_SCORE), :]
        acc = jnp.zeros((tq, ATT_CK_SCORE), F32)
        for j in range(H_IDX // 2):
            pair = qi_ref[0, :, j * LANES:(j + 1) * LANES]
            acc = acc + jnp.maximum(_dot_nt(pair, ka), 0.0) * head_weight(2 * j)
            acc = acc + jnp.maximum(_dot_nt(pair, kb), 0.0) * head_weight(2 * j + 1)
        admissible = start + lax.broadcasted_iota(I32, (tq, ATT_CK_SCORE), 1) <= row_pos
        masked = jnp.where(admissible, acc, -jnp.inf)
        sc_ref[:, pl.ds(start, ATT_CK_SCORE)] = masked
        return fold(jnp.maximum, hi_run, masked), fold(jnp.minimum, lo_run, jnp.where(admissible, acc, jnp.inf))

    hi_run, lo_run = _chunk_loop(nk * (ATT_CK // ATT_CK_SCORE), score,
                                 (jnp.full((tq, LANES), -jnp.inf, F32), jnp.full((tq, LANES), jnp.inf, F32)))

    thr = _select_threshold(sc_ref, thr_ref, nk, ATT_CK, n_sel, row_pos + 1, seq_len,
                            jnp.min(lo_run, axis=1, keepdims=True), jnp.max(hi_run, axis=1, keepdims=True))

    def to_mask(c, _):
        start = pl.multiple_of(c * ATT_CK, ATT_CK)
        sc_ref[:, pl.ds(start, ATT_CK)] = jnp.where(sc_ref[:, pl.ds(start, ATT_CK)] >= thr, 0.0, NEG_BIG)
        return 0

    lax.fori_loop(0, nk, to_mask, 0)

    rows = group * th

    def attend(u, _):
        r0 = pl.multiple_of(u // N_KV_HEADS * th, th)
        n = u % N_KV_HEADS
        kv_lanes = pl.ds(pl.multiple_of(n // 2 * LANES, LANES), LANES)
        qg = jnp.concatenate(
            [q_ref[0, pl.ds(r0, th), pl.ds(pl.multiple_of((group * n + g) * LANES, LANES), LANES)]
             for g in range(group)], axis=0)

        def logits(c, m_run):
            start = pl.multiple_of(c * ATT_CK, ATT_CK)
            s = (_dot_nt(qg, k_ref[0, pl.ds(start, ATT_CK), kv_lanes])
                 + jnp.concatenate([sc_ref[pl.ds(r0, th), pl.ds(start, ATT_CK)]] * group, axis=0))
            s_ref[:, pl.ds(start, ATT_CK)] = s
            return fold(jnp.maximum, m_run, s)

        m_run = _chunk_loop(nk, logits, jnp.full((rows, LANES), -jnp.inf, F32), unroll=ATT_UNROLL)
        m = jnp.max(m_run, axis=1, keepdims=True)

        def weigh(c, carry):
            l_run, acc = carry
            start = pl.multiple_of(c * ATT_CK, ATT_CK)
            p = jnp.exp(s_ref[:, pl.ds(start, ATT_CK)] - m)
            acc = acc + _dot(p.astype(BF16), v_ref[0, pl.ds(start, ATT_CK), kv_lanes])
            return fold(jnp.add, l_run, p), acc

        l_run, acc = _chunk_loop(nk, weigh, (jnp.zeros((rows, LANES), F32), jnp.zeros((rows, LANES), F32)),
                                 unroll=ATT_UNROLL)
        out = acc / jnp.sum(l_run, axis=1, keepdims=True)
        out = jnp.where(n % 2 == 1, out[:, HEAD_DIM:], out[:, :HEAD_DIM])
        width = group * HEAD_DIM
        o_ref[0, pl.ds(r0, th), pl.ds(pl.multiple_of(n * width, width), width)] = jnp.concatenate(
            [out[g * th:(g + 1) * th] for g in range(group)], axis=1).astype(o_ref.dtype)
        return 0

    lax.fori_loop(0, tq // th * N_KV_HEADS, attend, 0)


def _prompt_attention(qi, wi, kia, kib, qpad, kb, vb, *, n_sel):
    b, s, _ = qi.shape
    tq = min(ATT_TQ, s)
    th = min(ATT_TH, tq)
    assert s % ATT_CK == 0 and ATT_CK % tq == 0 and ATT_CK % ATT_CK_SCORE == 0 and tq % th == 0
    n_heads = qpad.shape[2] // LANES
    tile = lambda bi, i: (bi, i, 0)
    seq = lambda bi, i: (bi, 0, 0)
    return pl.pallas_call(
        functools.partial(_attn_kernel, tq=tq, th=th, n_sel=n_sel, seq_len=s),
        grid=(b, s // tq),
        in_specs=[pl.BlockSpec((1, tq, qi.shape[2]), tile), pl.BlockSpec((1, tq, wi.shape[2]), tile),
                  _resident((1, s, LANES), seq), _resident((1, s, LANES), seq),
                  pl.BlockSpec((1, tq, qpad.shape[2]), tile),
                  _resident((1, s, kb.shape[2]), seq), _resident((1, s, vb.shape[2]), seq)],
        out_specs=pl.BlockSpec((1, tq, n_heads * HEAD_DIM), tile),
        out_shape=jax.ShapeDtypeStruct((b, s, n_heads * HEAD_DIM), BF16),
        scratch_shapes=[pltpu.VMEM((tq, s), F32), pltpu.VMEM((tq, 1), F32),
                        pltpu.VMEM((n_heads // N_KV_HEADS * th, s), F32), pltpu.VMEM((H_IDX, tq, LANES), F32)],
        compiler_params=_cparams(("arbitrary", "arbitrary")),
        name="prompt_attention",
    )(qi, wi, kia, kib, qpad, kb, vb)


def _sample_select_kernel(pt_ref, qi_ref, w_ref, kin_ref, cache_ref, idx_ref, buf_ref, row_ref, sc_ref, thr_ref, sem,
                          *, layer, n_pool, n_pages, page, n_sel, rows):
    b = pl.program_id(0)
    past = n_pages * page
    total = rows * LANES

    def page_copy(j):
        return pltpu.make_async_copy(cache_ref.at[pl.ds((layer * n_pool + pt_ref[b, j]) * page, page)],
                                     buf_ref.at[pl.ds(j * page, page)], sem)

    def start(j, _):
        page_copy(j).start()
        return 0

    def wait(j, _):
        page_copy(j).wait()
        return 0

    lax.fori_loop(0, n_pages, start, 0)

    @pl.when(b == 0)
    def _():
        buf_ref[pl.ds(past, total - past), :] = jnp.zeros((total - past, D_IDX), F32)

    lax.fori_loop(0, n_pages, wait, 0)
    buf_ref[pl.ds(past, 1), :] = kin_ref[0]

    x = _dot_nt(qi_ref[0], buf_ref[...].astype(BF16))
    score = jnp.sum(jnp.maximum(x, 0.0) * w_ref[0], axis=0, keepdims=True)
    admissible = lax.broadcasted_iota(I32, (1, total), 1) <= past
    masked = jnp.where(admissible, score, -jnp.inf)
    row_ref[...] = masked
    for r in range(rows):
        sc_ref[r:r + 1, :] = row_ref[:, r * LANES:(r + 1) * LANES]
    thr = _select_threshold(sc_ref, thr_ref, 1, LANES, n_sel, jnp.full((1, 1), past + 1, I32), total,
                            jnp.min(jnp.where(admissible, score, jnp.inf), axis=1, keepdims=True),
                            jnp.max(masked, axis=1, keepdims=True), whole=True)
    keep = sc_ref[...] >= thr

    kb16 = jnp.where(keep, 1.0, 0.0).astype(BF16)
    li = lax.broadcasted_iota(I32, (LANES, LANES), 0)
    lj = lax.broadcasted_iota(I32, (LANES, LANES), 1)
    within = _dot(kb16, jnp.where(li <= lj, 1.0, 0.0).astype(BF16))
    rowsum = within[:, LANES - 1:LANES]
    ri = lax.broadcasted_iota(I32, (rows, rows), 0)
    rj = lax.broadcasted_iota(I32, (rows, rows), 1)
    incl = _dot(jnp.where(rj <= ri, 1.0, 0.0).astype(BF16),
                jnp.broadcast_to(rowsum, (rows, LANES)).astype(BF16))[:, 0:1]
    slot = lax.broadcasted_iota(I32, (rows, n_sel), 1).astype(F32)
    row_of = jnp.sum(jnp.where(incl <= slot, 1.0, 0.0), axis=0, keepdims=True)
    onehot = lax.broadcasted_iota(I32, (rows, n_sel), 0).astype(F32) == row_of
    row_off = jnp.sum(jnp.where(onehot, incl - rowsum, 0.0), axis=0, keepdims=True)
    rank = slot[0:1, :] - row_off + 1.0
    marked = jnp.where(keep, within, 0.0).astype(BF16)
    ranks_t = lax.dot_general(marked, jnp.where(onehot, 1.0, 0.0).astype(BF16),
                              (((0,), (0,)), ((), ())), preferred_element_type=F32)
    lane_id = lax.broadcasted_iota(I32, (LANES, n_sel), 0).astype(F32)
    lane_of = jnp.sum(jnp.where(ranks_t == rank, lane_id, 0.0), axis=0, keepdims=True)
    idx_ref[0] = (row_of * LANES + lane_of).astype(I32)


def _sample_select(page_table, qi, wi, ki_new, cache_idx, *, layer, n_sel):
    db, n_pages = page_table.shape
    depth, n_pool, page, _ = cache_idx.shape
    past = n_pages * page
    total = _round_up(past + 1, SUBLANES * LANES)
    assert total % ATT_CK == 0
    rows = total // LANES
    grid_spec = pltpu.PrefetchScalarGridSpec(
        num_scalar_prefetch=1,
        grid=(db,),
        in_specs=[pl.BlockSpec((1, H_IDX, D_IDX), lambda b, pt: (b, 0, 0)),
                  pl.BlockSpec((1, H_IDX, 1), lambda b, pt: (b, 0, 0)),
                  pl.BlockSpec((1, 1, D_IDX), lambda b, pt: (b, 0, 0)),
                  pl.BlockSpec(memory_space=pl.ANY)],
        out_specs=pl.BlockSpec((1, 1, n_sel), lambda b, pt: (b, 0, 0)),
        scratch_shapes=[pltpu.VMEM((total, D_IDX), F32), pltpu.VMEM((1, total), F32),
                        pltpu.VMEM((rows, LANES), F32), pltpu.VMEM((1, 1), F32), pltpu.SemaphoreType.DMA(())],
    )
    return pl.pallas_call(
        functools.partial(_sample_select_kernel, layer=layer, n_pool=n_pool, n_pages=n_pages, page=page, n_sel=n_sel,
                          rows=rows),
        grid_spec=grid_spec,
        out_shape=jax.ShapeDtypeStruct((db, 1, n_sel), I32),
        compiler_params=_cparams(("arbitrary",)),
        name="sample_select",
    )(page_table, qi, wi, ki_new, cache_idx.reshape(depth * n_pool * page, D_IDX))


def _sample_attend_kernel(idx_ref, pt_ref, q_ref, knew_ref, vnew_ref, ck_ref, cv_ref, o_ref, kbuf, vbuf, sem,
                          *, layer, n_pool, n_pages, page, n_sel):
    b = pl.program_id(0)
    past = n_pages * page
    n_heads = q_ref.shape[1]
    group = n_heads // N_KV_HEADS

    def dst(buf, j):
        return buf.at[pl.ds(j * N_KV_HEADS, N_KV_HEADS)]

    def cached_rows(j, i):
        p = jnp.minimum(i, past - 1)
        phys = pt_ref[b, p // page]
        slot = p % page
        src = pl.ds(((layer * n_pool + phys) * page + slot) * N_KV_HEADS, N_KV_HEADS)
        return (pltpu.make_async_copy(ck_ref.at[src], dst(kbuf, j), sem),
                pltpu.make_async_copy(cv_ref.at[src], dst(vbuf, j), sem))

    def start(j, _):
        i = idx_ref[b, j]

        @pl.when(i < past)
        def _():
            for cp in cached_rows(j, i):
                cp.start()

        @pl.when(i >= past)
        def _():
            pltpu.make_async_copy(knew_ref.at[b], dst(kbuf, j), sem).start()
            pltpu.make_async_copy(vnew_ref.at[b], dst(vbuf, j), sem).start()
        return 0

    def wait(j, _):
        for cp in cached_rows(j, idx_ref[b, j]):
            cp.wait()
        return 0

    lax.fori_loop(0, n_sel, start, 0)
    lax.fori_loop(0, n_sel, wait, 0)

    q = q_ref[0]
    kv_of_row = lax.broadcasted_iota(I32, (n_heads, 1), 0) // group
    s = jnp.zeros((n_heads, n_sel), F32)
    for n in range(N_KV_HEADS):
        k_n = kbuf[pl.ds(n, n_sel, stride=N_KV_HEADS), :].astype(BF16)
        s = jnp.where(kv_of_row == n, _dot_nt(q, k_n), s)
    p = jnp.exp(s - jnp.max(s, axis=1, keepdims=True))
    p = (p / jnp.sum(p, axis=1, keepdims=True)).astype(BF16)
    o = jnp.zeros((n_heads, HEAD_DIM), F32)
    for n in range(N_KV_HEADS):
        v_n = vbuf[pl.ds(n, n_sel, stride=N_KV_HEADS), :].astype(BF16)
        o = jnp.where(kv_of_row == n, _dot(p, v_n), o)
    o_ref[0] = o


def _sample_attend(idx, page_table, q, k_new, v_new, cache_k, cache_v, *, layer, n_sel):
    db, n_pages = page_table.shape
    depth, n_pool, page = cache_k.shape[:3]
    n_heads = q.shape[1]
    flat = lambda c: c.reshape(depth * n_pool * page * N_KV_HEADS, HEAD_DIM)
    grid_spec = pltpu.PrefetchScalarGridSpec(
        num_scalar_prefetch=2,
        grid=(db,),
        in_specs=[pl.BlockSpec((1, n_heads, HEAD_DIM), lambda b, ix, pt: (b, 0, 0)),
                  pl.BlockSpec(memory_space=pl.ANY), pl.BlockSpec(memory_space=pl.ANY),
                  pl.BlockSpec(memory_space=pl.ANY), pl.BlockSpec(memory_space=pl.ANY)],
        out_specs=pl.BlockSpec((1, n_heads, HEAD_DIM), lambda b, ix, pt: (b, 0, 0)),
        scratch_shapes=[pltpu.VMEM((n_sel * N_KV_HEADS, HEAD_DIM), F32),
                        pltpu.VMEM((n_sel * N_KV_HEADS, HEAD_DIM), F32),
                        pltpu.SemaphoreType.DMA(())],
    )
    return pl.pallas_call(
        functools.partial(_sample_attend_kernel, layer=layer, n_pool=n_pool, n_pages=n_pages, page=page, n_sel=n_sel),
        grid_spec=grid_spec,
        out_shape=jax.ShapeDtypeStruct((db, n_heads, HEAD_DIM), F32),
        compiler_params=_cparams(("arbitrary",)),
        name="sample_attend",
    )(idx, page_table, q, k_new, v_new, flat(cache_k), flat(cache_v))


def _pack_ffn(w_in, w_out):
    d, f2 = w_in.shape
    f = f2 // 2
    fp = _round_up(f, FFN_TF)
    pad = lambda w, axis: jnp.pad(w, [(0, fp - f) if a == axis else (0, 0) for a in range(2)])
    return (pad(w_in[:, :f], 1).astype(BF16), pad(w_in[:, f:], 1).astype(BF16), pad(w_out, 0).astype(BF16))


def _pack_proj(w_in, d_mix):
    d = w_in.shape[0]
    c_conv = d_mix // C_CONV_FRACTION
    d_attn = d_mix - c_conv
    n_heads = d_attn // HEAD_DIM
    kvw = N_KV_HEADS * HEAD_DIM
    sizes = [2 * c_conv, d_attn, kvw, kvw, H_IDX * D_IDX, D_IDX, H_IDX]
    parts, o = [], 0
    for sz in sizes:
        parts.append(w_in[:, o:o + sz])
        o += sz
    p_conv, q, k, v, qi, ki, wi = parts
    qh = q.reshape(d, n_heads, HEAD_DIM)
    zero = jnp.zeros_like(qh)
    odd = ((jnp.arange(n_heads) // (n_heads // N_KV_HEADS)) % 2 == 1)[None, :, None]
    qpad = jnp.concatenate([jnp.where(odd, zero, qh), jnp.where(odd, qh, zero)], axis=-1).reshape(d, n_heads * LANES)
    z = jnp.zeros((d, LANES - D_IDX), w_in.dtype)
    cols = [p_conv[:, :c_conv], p_conv[:, c_conv:], qpad, k, v, qi,
            jnp.concatenate([ki, z], axis=1), jnp.concatenate([z, ki], axis=1),
            jnp.concatenate([wi, jnp.zeros((d, LANES - H_IDX), w_in.dtype)], axis=1)]
    w_all = jnp.concatenate(cols, axis=1).astype(BF16)
    assert w_all.shape[1] == _proj_layout(d_mix)[1]
    return w_all


def _own_half(o_pad, n_heads):
    odd = ((jnp.arange(n_heads) // (n_heads // N_KV_HEADS)) % 2 == 1)[:, None]
    return jnp.where(odd, o_pad[..., HEAD_DIM:], o_pad[..., :HEAD_DIM])


def kernel(x_prompt, x_sample, cache_k, cache_v, cache_idx_k, state_conv, page_table, norm_ffn1, ffn1_w_in, ffn1_w_out, norm_mix, w_in, conv_w, conv_b, conv_ln_g, conv_ln_b, w_out, norm_ffn2, ffn2_w_in, ffn2_w_out, norm_final):
    depth = w_in.shape[0]
    bsz, seq, d = x_prompt.shape
    db, dseq, _ = x_sample.shape
    assert dseq == 1
    d_mix = d
    c_conv = d_mix // C_CONV_FRACTION
    n_heads = (d_mix - c_conv) // HEAD_DIM
    page = cache_k.shape[2]
    n_pages = page_table.shape[1]
    past = n_pages * page
    n_sel_p = min(TOPK_MAX, seq // 4)
    n_sel_s = min(TOPK_MAX, (past + dseq) // 4)

    xp = x_prompt.reshape(bsz * seq, d)
    xs = x_sample.reshape(db, d)
    g_final = norm_final.reshape(1, d)
    outs_p, outs_s = [], []

    for l in range(depth):
        f1 = _pack_ffn(ffn1_w_in[l], ffn1_w_out[l])
        f2 = _pack_ffn(ffn2_w_in[l], ffn2_w_out[l])
        w_all = _pack_proj(w_in[l], d_mix)
        conv_p = jnp.stack([conv_b[l], conv_ln_g[l], conv_ln_b[l]])
        wo = w_out[l].astype(BF16)
        woc, woa = wo[:c_conv], wo[c_conv:]
        g1, gm, g2 = norm_ffn1[l].reshape(1, d), norm_mix[l].reshape(1, d), norm_ffn2[l].reshape(1, d)
        last = g_final if l == depth - 1 else None

        xp = _ffn(xp, g1, *f1, name="ffn1_prompt")
        (conv_o, qpad, k_new, v_new, kb, vb, qi, kia, kib, ki_new, wi, u_tail) = _proj(
            xp, gm, w_all, conv_w[l], conv_p, seq_len=seq, name="proj_prompt")
        r3 = lambda a: a.reshape(bsz, seq, a.shape[-1])
        attn = _prompt_attention(r3(qi), r3(wi), r3(kia), r3(kib), r3(qpad), r3(kb), r3(vb), n_sel=n_sel_p)
        xp = _ffn(xp, g2, *f2, mix=(conv_o, attn.reshape(bsz * seq, -1), woc, woa), g_final=last,
                  name="ffn2_prompt")
        outs_p.append((k_new.reshape(bsz, seq, N_KV_HEADS, HEAD_DIM), v_new.reshape(bsz, seq, N_KV_HEADS, HEAD_DIM),
                       ki_new.reshape(bsz, seq, D_IDX), u_tail[:, CONV_HALO - (CONV_WIDTH - 1):, :]))

        state = state_conv[l].astype(F32)
        xs = _ffn(xs, g1, *f1, name="ffn1_sample")
        (conv_o, qpad, k_new, v_new, _, _, qi, _, _, ki_new, wi, u_new) = _proj(
            xs, gm, w_all, conv_w[l], conv_p, seq_len=1, state=jnp.swapaxes(state, 0, 1), name="proj_sample")
        idx = _sample_select(page_table, qi.reshape(db, H_IDX, D_IDX), wi.reshape(db, H_IDX, 1),
                             ki_new.reshape(db, 1, D_IDX), cache_idx_k, layer=l, n_sel=n_sel_s)
        attn = _sample_attend(idx.reshape(db, n_sel_s), page_table,
                              _own_half(qpad.reshape(db, n_heads, LANES), n_heads),
                              k_new.reshape(db, N_KV_HEADS, HEAD_DIM), v_new.reshape(db, N_KV_HEADS, HEAD_DIM),
                              cache_k, cache_v, layer=l, n_sel=n_sel_s)
        attn = attn.reshape(db, n_heads * HEAD_DIM).astype(BF16)
        xs = _ffn(xs, g2, *f2, mix=(conv_o, attn, woc, woa), g_final=last, name="ffn2_sample")
        outs_s.append((k_new.reshape(db, 1, N_KV_HEADS, HEAD_DIM), v_new.reshape(db, 1, N_KV_HEADS, HEAD_DIM),
                       ki_new.reshape(db, 1, D_IDX),
                       jnp.concatenate([state[:, 1:, :], u_new[:, None, :]], axis=1)))

    stack = lambda outs, i: jnp.stack([o[i] for o in outs])
    return (xp.reshape(bsz, seq, d), xs.reshape(db, 1, d),
            stack(outs_p, 0), stack(outs_p, 1), stack(outs_p, 2), stack(outs_p, 3),
            stack(outs_s, 0), stack(outs_s, 1), stack(outs_s, 2), stack(outs_s, 3))
```

```python
import functools

import jax
import jax.numpy as jnp
from jax import lax
from jax.experimental import pallas as pl
from jax.experimental.pallas import tpu as pltpu

F32 = jnp.float32
BF16 = jnp.bfloat16
I32 = jnp.int32

C_CONV_FRACTION = 2
CONV_WIDTH = 31
HEAD_DIM = 64
N_KV_HEADS = 4
H_IDX = 16
D_IDX = 64
TOPK_MAX = 256
EPS = 1e-6
ATTN_SCALE = HEAD_DIM ** -0.5
IDX_SCALE = (D_IDX ** -0.5) * (H_IDX ** -0.5)

LANES = 128
SUBLANES = 8
VMEM_LIMIT_BYTES = 56 * 1024 * 1024

FFN_TM = 512
FFN_TF = 512
PROJ_TM = 256
CONV_HALO = 32
ATT_TQ = 128
ATT_TH = 128
ATT_CK_SCORE = 256
ATT_CK = 512
ATT_UNROLL = 4
SEARCH_CAP = 40

INT_MIN = -2 ** 31
NEG_BIG = -1e30
F32_LOWEST = float(jnp.finfo(jnp.float32).min)


def _round_up(x, m):
    return (x + m - 1) // m * m


def _cparams(sem):
    return pltpu.CompilerParams(dimension_semantics=sem, vmem_limit_bytes=VMEM_LIMIT_BYTES)


def _resident(shape, index_map):
    return pl.BlockSpec(shape, index_map, pipeline_mode=pl.Buffered(1))


def _rms(x, g):
    ms = jnp.mean(x * x, axis=-1, keepdims=True)
    return x * lax.rsqrt(ms + EPS) * g


def _dot(a, b):
    return jnp.dot(a, b, preferred_element_type=F32)


def _dot_nt(a, b):
    return lax.dot_general(a, b, (((1,), (1,)), ((), ())), preferred_element_type=F32)


def _ffn_kernel(*refs, has_mix, has_final):
    it = iter(refs)
    x_ref = next(it)
    if has_mix:
        mc_ref, ma_ref, woc_ref, woa_ref = next(it), next(it), next(it), next(it)
    g_ref, wa_ref, wb_ref, wo_ref = next(it), next(it), next(it), next(it)
    if has_final:
        gf_ref = next(it)
    o_ref = next(it)
    xn_ref = next(it)

    f = pl.program_id(1)

    @pl.when(f == 0)
    def _():
        x = x_ref[...]
        if has_mix:
            x = x + _dot(mc_ref[...], woc_ref[...]) + _dot(ma_ref[...], woa_ref[...])
        o_ref[...] = x
        xn_ref[...] = _rms(x, g_ref[...]).astype(BF16)

    xn = xn_ref[...]
    a = _dot(xn, wa_ref[...])
    b = _dot(xn, wb_ref[...])
    act = (a * jax.nn.sigmoid(a) * b).astype(BF16)
    o_ref[...] += 0.5 * _dot(act, wo_ref[...])

    if has_final:
        @pl.when(f == pl.num_programs(1) - 1)
        def _():
            o_ref[...] = _rms(o_ref[...], gf_ref[...])


def _ffn(x, g, wa, wb, wo, *, mix=None, g_final=None, name):
    m, d = x.shape
    fp = wa.shape[1]
    tm = min(FFN_TM, m)
    assert m % tm == 0 and fp % FFN_TF == 0
    nf = fp // FFN_TF
    has_mix = mix is not None
    has_final = g_final is not None

    row = lambda i, f: (i, 0)
    const = lambda i, f: (0, 0)
    args = [x]
    specs = [pl.BlockSpec((tm, d), row)]
    if has_mix:
        mc, ma, woc, woa = mix
        args += [mc, ma, woc, woa]
        specs += [pl.BlockSpec((tm, mc.shape[1]), row), pl.BlockSpec((tm, ma.shape[1]), row),
                  _resident(woc.shape, const), _resident(woa.shape, const)]
    args += [g, wa, wb, wo]
    specs += [_resident((1, d), const),
              pl.BlockSpec((d, FFN_TF), lambda i, f: (0, f)),
              pl.BlockSpec((d, FFN_TF), lambda i, f: (0, f)),
              pl.BlockSpec((FFN_TF, d), lambda i, f: (f, 0))]
    if has_final:
        args.append(g_final)
        specs.append(_resident((1, d), const))

    return pl.pallas_call(
        functools.partial(_ffn_kernel, has_mix=has_mix, has_final=has_final),
        grid=(m // tm, nf),
        in_specs=specs,
        out_specs=pl.BlockSpec((tm, d), row),
        out_shape=jax.ShapeDtypeStruct((m, d), F32),
        scratch_shapes=[pltpu.VMEM((tm, d), BF16)],
        compiler_params=_cparams(("arbitrary", "arbitrary")),
        name=name,
    )(*args)


def _proj_layout(d_mix):
    c_conv = d_mix // C_CONV_FRACTION
    d_attn = d_mix - c_conv
    n_heads = d_attn // HEAD_DIM
    widths = dict(ca=c_conv, cg=c_conv, qpad=n_heads * LANES, k=N_KV_HEADS * HEAD_DIM,
                  v=N_KV_HEADS * HEAD_DIM, qi=H_IDX * D_IDX, kia=LANES, kib=LANES, wi=LANES)
    off, o = {}, 0
    for name, w in widths.items():
        off[name] = (o, o + w)
        o += w
    return off, o


def _proj_kernel(*refs, tm, tiles_per_seq, sample, d_mix):
    it = iter(refs)
    x_ref, g_ref, w_ref, cw_ref, cp_ref = next(it), next(it), next(it), next(it), next(it)
    if sample:
        st_ref = next(it)
    (conv_ref, qpad_ref, k_ref, v_ref, kb_ref, vb_ref, qi_ref, kia_ref, kib_ref, ki_ref, wi_ref,
     u_ref) = (next(it) for _ in range(12))
    if not sample:
        win_ref, y_ref, z_ref = next(it), next(it), next(it)

    off, _ = _proj_layout(d_mix)
    c_conv = d_mix // C_CONV_FRACTION

    def cols(name):
        lo, hi = off[name]
        return w_ref[:, lo:hi]

    if not sample:
        @pl.when(pl.program_id(0) % tiles_per_seq == 0)
        def _():
            win_ref[0:CONV_HALO, :] = jnp.zeros((CONV_HALO, c_conv), F32)

    xn = _rms(x_ref[...], g_ref[...]).astype(BF16)
    u = _dot(xn, cols("ca")) * jax.nn.sigmoid(_dot(xn, cols("cg")))

    qpad_ref[...] = (_dot(xn, cols("qpad")) * ATTN_SCALE).astype(BF16)
    kk = _dot(xn, cols("k"))
    vv = _dot(xn, cols("v"))
    k_ref[...] = kk
    v_ref[...] = vv
    kb_ref[...] = kk.astype(BF16)
    vb_ref[...] = vv.astype(BF16)
    qi_ref[...] = _dot(xn, cols("qi")).astype(BF16)
    kia = _dot(xn, cols("kia"))
    kia_ref[...] = kia.astype(BF16)
    kib_ref[...] = _dot(xn, cols("kib")).astype(BF16)
    ki_ref[...] = kia[:, :D_IDX]
    wi_ref[...] = _dot(xn, cols("wi"))[:, :H_IDX] * IDX_SCALE

    bias = cp_ref[0:1, :]
    ln_g = cp_ref[1:2, :]
    ln_b = cp_ref[2:3, :]

    if sample:
        u_ref[...] = u
        y = bias + cw_ref[CONV_WIDTH - 1:CONV_WIDTH, :] * u
        for j in range(CONV_WIDTH - 1):
            y = y + cw_ref[j:j + 1, :] * st_ref[j]
    else:
        win_ref[CONV_HALO:CONV_HALO + tm, :] = u
        first = CONV_HALO - (CONV_WIDTH - 1)
        for c in range(c_conv // LANES):
            cs = slice(c * LANES, (c + 1) * LANES)
            acc = jnp.zeros((tm, LANES), F32) + bias[:, cs]
            for r in range(SUBLANES):
                taps = [j for j in range(CONV_WIDTH) if (first + j) % SUBLANES == r]
                if taps:
                    base = first + taps[0]
                    span = first + taps[-1] + tm - base
                    if r:
                        z_ref[0:span, :] = win_ref[base:base + span, cs]
                    for j in taps:
                        lo = first + j - base
                        z = z_ref[lo:lo + tm, :] if r else win_ref[first + j:first + j + tm, cs]
                        acc = acc + cw_ref[j:j + 1, cs] * z
            y_ref[:, cs] = acc
        y = y_ref[...]
        tail = win_ref[tm:tm + CONV_HALO, :]
        u_ref[0] = tail
        win_ref[0:CONV_HALO, :] = tail

    mu = jnp.mean(y, axis=-1, keepdims=True)
    var = jnp.mean(jnp.square(y - mu), axis=-1, keepdims=True)
    yn = (y - mu) * lax.rsqrt(var + EPS) * ln_g + ln_b
    conv_ref[...] = (yn * jax.nn.sigmoid(yn)).astype(BF16)


def _proj(x, g, w_all, conv_w, conv_p, *, seq_len, state=None, name):
    m, d = x.shape
    d_mix = d
    c_conv = d_mix // C_CONV_FRACTION
    n_heads = (d_mix - c_conv) // HEAD_DIM
    kvw = N_KV_HEADS * HEAD_DIM
    sample = state is not None
    tm = m if sample else min(PROJ_TM, seq_len)
    assert m % tm == 0 and seq_len % tm == 0 or sample
    assert tm >= CONV_HALO or sample
    nt = m // tm
    tiles_per_seq = max(seq_len // tm, 1)
    n_seq = m // seq_len

    row = lambda i: (i, 0)
    const = lambda i: (0, 0)
    args = [x, g, w_all, conv_w, conv_p]
    specs = [pl.BlockSpec((tm, d), row), _resident((1, d), const), _resident(w_all.shape, const),
             _resident(conv_w.shape, const), _resident(conv_p.shape, const)]
    if sample:
        args.append(state)
        specs.append(_resident(state.shape, lambda i: (0, 0, 0)))

    def out(width, dtype):
        return jax.ShapeDtypeStruct((m, width), dtype), pl.BlockSpec((tm, width), row)

    outs = [out(c_conv, BF16), out(n_heads * LANES, BF16), out(kvw, F32), out(kvw, F32),
            out(kvw, BF16), out(kvw, BF16), out(H_IDX * D_IDX, BF16), out(LANES, BF16),
            out(LANES, BF16), out(D_IDX, F32), out(H_IDX, F32)]
    if sample:
        outs.append(out(c_conv, F32))
        scratch = []
    else:
        outs.append((jax.ShapeDtypeStruct((n_seq, CONV_HALO, c_conv), F32),
                     pl.BlockSpec((1, CONV_HALO, c_conv), lambda i: (i // tiles_per_seq, 0, 0))))
        scratch = [pltpu.VMEM((tm + CONV_HALO, c_conv), F32), pltpu.VMEM((tm, c_conv), F32),
                   pltpu.VMEM((tm + CONV_HALO, LANES), F32)]

    return pl.pallas_call(
        functools.partial(_proj_kernel, tm=tm, tiles_per_seq=tiles_per_seq, sample=sample, d_mix=d_mix),
        grid=(nt,),
        in_specs=specs,
        out_specs=[o[1] for o in outs],
        out_shape=[o[0] for o in outs],
        scratch_shapes=scratch,
        compiler_params=_cparams(("arbitrary",)),
        name=name,
    )(*args)


def _key_to_f32(key):
    bits = key ^ ((key >> 31) & jnp.int32(0x7FFFFFFF))
    return lax.bitcast_convert_type(bits, F32)


def _chunk_loop(n, body, carry, unroll=2):
    assert unroll & (unroll - 1) == 0

    def run(first, count, carry):
        for u in range(count):
            carry = body(first + u, carry)
        return carry

    carry = lax.fori_loop(0, n // unroll, lambda i, c: run(i * unroll, unroll, c), carry)
    done = n // unroll * unroll
    part = unroll // 2
    while part >= 1:
        carry = lax.cond((n - done) // part % 2 == 1, lambda c, d=done, p=part: run(d, p, c), lambda c: c, carry)
        done = done + jnp.where((n - done) // part % 2 == 1, part, 0)
        part //= 2
    return carry


def _count(sc_ref, nchunks, ck, pred, whole=False):
    rows = sc_ref.shape[0]

    def body(c, acc):
        start = pl.multiple_of(c * ck, ck)
        hit = jnp.where(pred(sc_ref[:, pl.ds(start, ck)], start), 1.0, 0.0)
        part = hit[:, 0:LANES]
        for i in range(1, ck // LANES):
            part = part + hit[:, i * LANES:(i + 1) * LANES]
        return acc + part

    acc = lax.fori_loop(0, nchunks, body, jnp.zeros((rows, LANES), F32))
    per_row = jnp.sum(acc, axis=1, keepdims=True)
    return jnp.sum(per_row, axis=0, keepdims=True) if whole else per_row


def _position(shape, start, whole):
    col = start + lax.broadcasted_iota(I32, shape, 1)
    return lax.broadcasted_iota(I32, shape, 0) * shape[1] + col if whole else col


def _exact_threshold(sc_ref, nchunks, ck, n_sel, n_adm, total_cols, whole=False):
    rows = 1 if whole else sc_ref.shape[0]
    want = jnp.float32(n_sel)

    def bisect(i, carry):
        t, ct = carry
        cand = t + lax.shift_left(jnp.int32(1), 31 - i)
        thr = _key_to_f32(cand)
        cnt = _count(sc_ref, nchunks, ck, lambda blk, _: blk >= thr, whole)
        take = cnt >= want
        return jnp.where(take, cand, t), jnp.where(take, cnt, ct)

    t, ct = lax.fori_loop(0, 32, bisect, (jnp.full((rows, 1), INT_MIN, I32), jnp.zeros((rows, 1), F32)))
    full = n_adm <= n_sel
    thr = jnp.where(full, F32_LOWEST, _key_to_f32(t))
    tied = jnp.logical_and(jnp.logical_not(full), ct > want)

    @pl.when(jnp.sum(jnp.where(tied, 1.0, 0.0)) > 0.0)
    def _():
        n_gt = _count(sc_ref, nchunks, ck, lambda blk, _: blk > thr, whole)
        room = want - n_gt
        nbits = max(int(total_cols - 1).bit_length(), 1)

        def search(i, q):
            cand = q + lax.shift_left(jnp.int32(1), nbits - 1 - i)
            below = _count(sc_ref, nchunks, ck,
                           lambda blk, s: jnp.logical_and(blk == thr, _position(blk.shape, s, whole) < cand), whole)
            return jnp.where(below < room, cand, q)

        last = lax.fori_loop(0, nbits, search, jnp.zeros((rows, 1), I32))

        def drop(c, _):
            start = pl.multiple_of(c * ck, ck)
            blk = sc_ref[:, pl.ds(start, ck)]
            lose = jnp.logical_and(tied, jnp.logical_and(blk == thr, _position(blk.shape, start, whole) > last))
            sc_ref[:, pl.ds(start, ck)] = jnp.where(lose, -jnp.inf, blk)
            return 0

        lax.fori_loop(0, nchunks, drop, 0)

    return thr


def _select_threshold(sc_ref, thr_ref, nchunks, ck, n_sel, n_adm, total_cols, row_lo, row_hi, whole=False):
    rows = 1 if whole else sc_ref.shape[0]
    want = jnp.float32(n_sel)
    full = n_adm <= n_sel

    def unresolved(done):
        return jnp.sum(done) < rows

    def cond(state):
        it, go = state[0], state[1]
        return jnp.logical_and(it < SEARCH_CAP, go)

    def body(state):
        it, _, lo, hi, thr, done = state
        mid = lo + (hi - lo) * 0.5
        cnt = _count(sc_ref, nchunks, ck, lambda blk, _: blk >= mid, whole)
        go = unresolved(done)
        found = cnt == want
        thr = jnp.where(jnp.logical_and(found, done == 0.0), mid, thr)
        done = jnp.where(found, 1.0, done)
        return it + 1, go, jnp.where(cnt > want, mid, lo), jnp.where(cnt < want, mid, hi), thr, done

    done0 = jnp.where(full, 1.0, 0.0)
    init = (jnp.int32(0), unresolved(done0), row_lo, row_hi, jnp.where(full, F32_LOWEST, row_lo), done0)
    _, _, _, _, thr, done = lax.while_loop(cond, body, init)
    thr_ref[...] = thr

    @pl.when(unresolved(done))
    def _():
        exact = _exact_threshold(sc_ref, nchunks, ck, n_sel, n_adm, total_cols, whole)
        thr_ref[...] = jnp.where(done > 0.0, thr, exact)

    return thr_ref[...]


def _attn_kernel(qi_ref, w_ref, kia_ref, kib_ref, q_ref, k_ref, v_ref, o_ref, sc_ref, thr_ref, s_ref, wb_ref,
                 *, tq, th, n_sel, seq_len):
    t0 = pl.program_id(1) * tq
    nk = t0 // ATT_CK + 1
    row_pos = t0 + lax.broadcasted_iota(I32, (tq, 1), 0)
    n_heads = q_ref.shape[2] // LANES
    group = n_heads // N_KV_HEADS

    def fold(op, acc, x):
        for i in range(x.shape[1] // LANES):
            acc = op(acc, x[:, i * LANES:(i + 1) * LANES])
        return acc

    for h in range(H_IDX):
        wb_ref[h] = jnp.broadcast_to(w_ref[0, :, h:h + 1], (tq, LANES))

    def head_weight(h):
        return jnp.concatenate([wb_ref[h]] * (ATT_CK_SCORE // LANES), axis=1)

    def score(c, carry):
        hi_run, lo_run = carry
        start = pl.multiple_of(c * ATT_CK_SCORE, ATT_CK_SCORE)
        ka = kia_ref[0, pl.ds(start, ATT_CK_SCORE), :]
        kb = kib_ref[0, pl.ds(start, ATT_CK_SCORE), :]
        acc = jnp.zeros((tq, ATT_CK_SCORE), F32)
        for j in range(H_IDX // 2):
            pair = qi_ref[0, :, j * LANES:(j + 1) * LANES]
            acc = acc + jnp.maximum(_dot_nt(pair, ka), 0.0) * head_weight(2 * j)
            acc = acc + jnp.maximum(_dot_nt(pair, kb), 0.0) * head_weight(2 * j + 1)
        admissible = start + lax.broadcasted_iota(I32, (tq, ATT_CK_SCORE), 1) <= row_pos
        masked = jnp.where(admissible, acc, -jnp.inf)
        sc_ref[:, pl.ds(start, ATT_CK_SCORE)] = masked
        return fold(jnp.maximum, hi_run, masked), fold(jnp.minimum, lo_run, jnp.where(admissible, acc, jnp.inf))

    hi_run, lo_run = _chunk_loop(nk * (ATT_CK // ATT_CK_SCORE), score,
                                 (jnp.full((tq, LANES), -jnp.inf, F32), jnp.full((tq, LANES), jnp.inf, F32)))

    thr = _select_threshold(sc_ref, thr_ref, nk, ATT_CK, n_sel, row_pos + 1, seq_len,
                            jnp.min(lo_run, axis=1, keepdims=True), jnp.max(hi_run, axis=1, keepdims=True))

    def to_mask(c, _):
        start = pl.multiple_of(c * ATT_CK, ATT_CK)
        sc_ref[:, pl.ds(start, ATT_CK)] = jnp.where(sc_ref[:, pl.ds(start, ATT_CK)] >= thr, 0.0, NEG_BIG)
        return 0

    lax.fori_loop(0, nk, to_mask, 0)

    rows = group * th

    def attend(u, _):
        r0 = pl.multiple_of(u // N_KV_HEADS * th, th)
        n = u % N_KV_HEADS
        kv_lanes = pl.ds(pl.multiple_of(n // 2 * LANES, LANES), LANES)
        qg = jnp.concatenate(
            [q_ref[0, pl.ds(r0, th), pl.ds(pl.multiple_of((group * n + g) * LANES, LANES), LANES)]
             for g in range(group)], axis=0)

        def logits(c, m_run):
            start = pl.multiple_of(c * ATT_CK, ATT_CK)
            s = (_dot_nt(qg, k_ref[0, pl.ds(start, ATT_CK), kv_lanes])
                 + jnp.concatenate([sc_ref[pl.ds(r0, th), pl.ds(start, ATT_CK)]] * group, axis=0))
            s_ref[:, pl.ds(start, ATT_CK)] = s
            return fold(jnp.maximum, m_run, s)

        m_run = _chunk_loop(nk, logits, jnp.full((rows, LANES), -jnp.inf, F32), unroll=ATT_UNROLL)
        m = jnp.max(m_run, axis=1, keepdims=True)

        def weigh(c, carry):
            l_run, acc = carry
            start = pl.multiple_of(c * ATT_CK, ATT_CK)
            p = jnp.exp(s_ref[:, pl.ds(start, ATT_CK)] - m)
            acc = acc + _dot(p.astype(BF16), v_ref[0, pl.ds(start, ATT_CK), kv_lanes])
            return fold(jnp.add, l_run, p), acc

        l_run, acc = _chunk_loop(nk, weigh, (jnp.zeros((rows, LANES), F32), jnp.zeros((rows, LANES), F32)),
                                 unroll=ATT_UNROLL)
        out = acc / jnp.sum(l_run, axis=1, keepdims=True)
        out = jnp.where(n % 2 == 1, out[:, HEAD_DIM:], out[:, :HEAD_DIM])
        width = group * HEAD_DIM
        o_ref[0, pl.ds(r0, th), pl.ds(pl.multiple_of(n * width, width), width)] = jnp.concatenate(
            [out[g * th:(g + 1) * th] for g in range(group)], axis=1).astype(o_ref.dtype)
        return 0

    lax.fori_loop(0, tq // th * N_KV_HEADS, attend, 0)


def _prompt_attention(qi, wi, kia, kib, qpad, kb, vb, *, n_sel):
    b, s, _ = qi.shape
    tq = min(ATT_TQ, s)
    th = min(ATT_TH, tq)
    assert s % ATT_CK == 0 and ATT_CK % tq == 0 and ATT_CK % ATT_CK_SCORE == 0 and tq % th == 0
    n_heads = qpad.shape[2] // LANES
    tile = lambda bi, i: (bi, i, 0)
    seq = lambda bi, i: (bi, 0, 0)
    return pl.pallas_call(
        functools.partial(_attn_kernel, tq=tq, th=th, n_sel=n_sel, seq_len=s),
        grid=(b, s // tq),
        in_specs=[pl.BlockSpec((1, tq, qi.shape[2]), tile), pl.BlockSpec((1, tq, wi.shape[2]), tile),
                  _resident((1, s, LANES), seq), _resident((1, s, LANES), seq),
                  pl.BlockSpec((1, tq, qpad.shape[2]), tile),
                  _resident((1, s, kb.shape[2]), seq), _resident((1, s, vb.shape[2]), seq)],
        out_specs=pl.BlockSpec((1, tq, n_heads * HEAD_DIM), tile),
        out_shape=jax.ShapeDtypeStruct((b, s, n_heads * HEAD_DIM), BF16),
        scratch_shapes=[pltpu.VMEM((tq, s), F32), pltpu.VMEM((tq, 1), F32),
                        pltpu.VMEM((n_heads // N_KV_HEADS * th, s), F32), pltpu.VMEM((H_IDX, tq, LANES), F32)],
        compiler_params=_cparams(("arbitrary", "arbitrary")),
        name="prompt_attention",
    )(qi, wi, kia, kib, qpad, kb, vb)


def _native_pages(cache):
    rank = cache.ndim
    t = jnp.transpose(cache, (0, 1) + tuple(range(3, rank)) + (2,))
    return t.reshape(-1, cache.shape[2])


def _slot0_page(x, page):
    return jnp.pad(x[:, :, None], ((0, 0), (0, 0), (0, page - 1)))


def _sample_select_kernel(pt_ref, qi_ref, w_ref, kin_ref, cache_ref, mask_ref, buf_ref, row_ref, sc_ref, thr_ref, sem,
                          *, layer, n_pool, n_pages, page, n_sel, rows):
    b = pl.program_id(0)
    past = n_pages * page
    total = rows * LANES

    def page_copy(j):
        src = pl.ds((layer * n_pool + pt_ref[b, j]) * D_IDX, D_IDX)
        return pltpu.make_async_copy(cache_ref.at[src], buf_ref.at[:, pl.ds(j * page, page)], sem)

    def start(j, _):
        page_copy(j).start()
        return 0

    def wait(j, _):
        page_copy(j).wait()
        return 0

    lax.fori_loop(0, n_pages, start, 0)

    @pl.when(b == 0)
    def _():
        buf_ref[:, past + page:] = jnp.zeros((D_IDX, total - past - page), F32)

    buf_ref[:, past:past + page] = kin_ref[0]
    lax.fori_loop(0, n_pages, wait, 0)

    x = _dot(qi_ref[0], buf_ref[...].astype(BF16))
    score = jnp.sum(jnp.maximum(x, 0.0) * w_ref[0], axis=0, keepdims=True)
    admissible = lax.broadcasted_iota(I32, (1, total), 1) <= past
    masked = jnp.where(admissible, score, -jnp.inf)
    row_ref[...] = masked
    for r in range(rows):
        sc_ref[r:r + 1, :] = row_ref[:, r * LANES:(r + 1) * LANES]
    thr = _select_threshold(sc_ref, thr_ref, 1, LANES, n_sel, jnp.full((1, 1), past + 1, I32), total,
                            jnp.min(jnp.where(admissible, score, jnp.inf), axis=1, keepdims=True),
                            jnp.max(masked, axis=1, keepdims=True), whole=True)
    mask_ref[0] = jnp.where(sc_ref[...] >= thr, 0.0, NEG_BIG)


def _sample_select(page_table, qi, wi, ki_new, cache_idx, *, layer, n_sel):
    db, n_pages = page_table.shape
    depth, n_pool, page, _ = cache_idx.shape
    assert page == LANES
    past = n_pages * page
    total = _round_up(past + page, SUBLANES * LANES)
    rows = total // LANES
    grid_spec = pltpu.PrefetchScalarGridSpec(
        num_scalar_prefetch=1,
        grid=(db,),
        in_specs=[pl.BlockSpec((1, H_IDX, D_IDX), lambda b, pt: (b, 0, 0)),
                  pl.BlockSpec((1, H_IDX, 1), lambda b, pt: (b, 0, 0)),
                  pl.BlockSpec((1, D_IDX, page), lambda b, pt: (b, 0, 0)),
                  pl.BlockSpec(memory_space=pl.ANY)],
        out_specs=pl.BlockSpec((1, rows, LANES), lambda b, pt: (b, 0, 0)),
        scratch_shapes=[pltpu.VMEM((D_IDX, total), F32), pltpu.VMEM((1, total), F32),
                        pltpu.VMEM((rows, LANES), F32), pltpu.VMEM((1, 1), F32), pltpu.SemaphoreType.DMA(())],
    )
    return pl.pallas_call(
        functools.partial(_sample_select_kernel, layer=layer, n_pool=n_pool, n_pages=n_pages, page=page, n_sel=n_sel,
                          rows=rows),
        grid_spec=grid_spec,
        out_shape=jax.ShapeDtypeStruct((db, rows, LANES), F32),
        compiler_params=_cparams(("arbitrary",)),
        name="sample_select",
    )(page_table, qi, wi, _slot0_page(ki_new, page), _native_pages(cache_idx))


def _sample_attend_kernel(pt_ref, mask_ref, qt_ref, knew_ref, vnew_ref, ck_ref, cv_ref, o_ref,
                          kbuf, vbuf, s_ref, qb_ref, sem_k, sem_v, *, layer, n_pool, n_pages, page, rows):
    b = pl.program_id(0)
    n_heads = qt_ref.shape[2]
    group = n_heads // N_KV_HEADS
    kvw = N_KV_HEADS * HEAD_DIM
    live = n_pages + 1

    def page_copies(j):
        src = pl.ds((layer * n_pool + pt_ref[b, j]) * kvw, kvw)
        dst = pl.ds(j * kvw, kvw)
        return (pltpu.make_async_copy(ck_ref.at[src], kbuf.at[dst], sem_k),
                pltpu.make_async_copy(cv_ref.at[src], vbuf.at[dst], sem_v))

    def start(j, _):
        for cp in page_copies(j):
            cp.start()
        return 0

    lax.fori_loop(0, n_pages, start, 0)
    new_dst = pl.ds(n_pages * kvw, kvw)
    new_k = pltpu.make_async_copy(knew_ref.at[b], kbuf.at[new_dst], sem_k)
    new_v = pltpu.make_async_copy(vnew_ref.at[b], vbuf.at[new_dst], sem_v)
    new_k.start()
    new_v.start()

    for h in range(n_heads):
        qb_ref[h] = jnp.broadcast_to(qt_ref[0, :, h:h + 1], (HEAD_DIM, LANES))

    s_ref[:, live * page:] = jnp.full((n_heads, (rows - live) * page), NEG_BIG, F32)

    def wait_k(j, _):
        page_copies(j)[0].wait()
        return 0

    lax.fori_loop(0, n_pages, wait_k, 0)
    new_k.wait()

    def logits(j, _):
        row0 = pl.multiple_of(j * kvw, kvw)
        col0 = pl.multiple_of(j * page, page)
        drop = mask_ref[0, pl.ds(j, 1), :]
        for n in range(N_KV_HEADS):
            k_n = kbuf[pl.ds(row0 + n * HEAD_DIM, HEAD_DIM), :]
            for g in range(group):
                h = n * group + g
                s_ref[h:h + 1, pl.ds(col0, page)] = jnp.sum(k_n * qb_ref[h], axis=0, keepdims=True) + drop
        return 0

    lax.fori_loop(0, live, logits, 0)

    s = s_ref[...]
    p = jnp.exp(s - jnp.max(s, axis=1, keepdims=True))
    s_ref[...] = p / jnp.sum(p, axis=1, keepdims=True)

    def wait_v(j, _):
        page_copies(j)[1].wait()
        return 0

    lax.fori_loop(0, n_pages, wait_v, 0)
    new_v.wait()

    for n in range(N_KV_HEADS):
        def weigh(j, accs):
            row0 = pl.multiple_of(j * kvw, kvw)
            col0 = pl.multiple_of(j * page, page)
            v_n = vbuf[pl.ds(row0 + n * HEAD_DIM, HEAD_DIM), :]
            return tuple(acc + v_n * s_ref[n * group + g:n * group + g + 1, pl.ds(col0, page)]
                         for g, acc in enumerate(accs))

        accs = lax.fori_loop(0, live, weigh, tuple(jnp.zeros((HEAD_DIM, LANES), F32) for _ in range(group)))
        for g, acc in enumerate(accs):
            h = n * group + g
            o_ref[0, :, h:h + 1] = jnp.sum(acc, axis=1, keepdims=True)


def _sample_attend(page_table, mask, q, k_new, v_new, cache_k, cache_v, *, layer):
    db, n_pages = page_table.shape
    depth, n_pool, page = cache_k.shape[:3]
    n_heads = q.shape[1]
    rows = mask.shape[1]
    kvw = N_KV_HEADS * HEAD_DIM
    assert page == LANES and rows > n_pages
    grid_spec = pltpu.PrefetchScalarGridSpec(
        num_scalar_prefetch=1,
        grid=(db,),
        in_specs=[pl.BlockSpec((1, rows, LANES), lambda b, pt: (b, 0, 0)),
                  pl.BlockSpec((1, HEAD_DIM, n_heads), lambda b, pt: (b, 0, 0)),
                  pl.BlockSpec(memory_space=pl.ANY), pl.BlockSpec(memory_space=pl.ANY),
                  pl.BlockSpec(memory_space=pl.ANY), pl.BlockSpec(memory_space=pl.ANY)],
        out_specs=pl.BlockSpec((1, HEAD_DIM, n_heads), lambda b, pt: (b, 0, 0)),
        scratch_shapes=[pltpu.VMEM(((n_pages + 1) * kvw, page), F32), pltpu.VMEM(((n_pages + 1) * kvw, page), F32),
                        pltpu.VMEM((n_heads, rows * LANES), F32), pltpu.VMEM((n_heads, HEAD_DIM, LANES), F32),
                        pltpu.SemaphoreType.DMA(()), pltpu.SemaphoreType.DMA(())],
    )
    o_t = pl.pallas_call(
        functools.partial(_sample_attend_kernel, layer=layer, n_pool=n_pool, n_pages=n_pages, page=page, rows=rows),
        grid_spec=grid_spec,
        out_shape=jax.ShapeDtypeStruct((db, HEAD_DIM, n_heads), F32),
        compiler_params=_cparams(("arbitrary",)),
        name="sample_attend",
    )(page_table, mask, jnp.swapaxes(q, 1, 2).astype(F32), _slot0_page(k_new, page), _slot0_page(v_new, page),
      _native_pages(cache_k), _native_pages(cache_v))
    return jnp.swapaxes(o_t, 1, 2).reshape(db, n_heads * HEAD_DIM)


def _pack_ffn(w_in, w_out):
    d, f2 = w_in.shape
    f = f2 // 2
    fp = _round_up(f, FFN_TF)
    pad = lambda w, axis: jnp.pad(w, [(0, fp - f) if a == axis else (0, 0) for a in range(2)])
    return (pad(w_in[:, :f], 1).astype(BF16), pad(w_in[:, f:], 1).astype(BF16), pad(w_out, 0).astype(BF16))


def _pack_proj(w_in, d_mix):
    d = w_in.shape[0]
    c_conv = d_mix // C_CONV_FRACTION
    d_attn = d_mix - c_conv
    n_heads = d_attn // HEAD_DIM
    kvw = N_KV_HEADS * HEAD_DIM
    sizes = [2 * c_conv, d_attn, kvw, kvw, H_IDX * D_IDX, D_IDX, H_IDX]
    parts, o = [], 0
    for sz in sizes:
        parts.append(w_in[:, o:o + sz])
        o += sz
    p_conv, q, k, v, qi, ki, wi = parts
    qh = q.reshape(d, n_heads, HEAD_DIM)
    zero = jnp.zeros_like(qh)
    odd = ((jnp.arange(n_heads) // (n_heads // N_KV_HEADS)) % 2 == 1)[None, :, None]
    qpad = jnp.concatenate([jnp.where(odd, zero, qh), jnp.where(odd, qh, zero)], axis=-1).reshape(d, n_heads * LANES)
    z = jnp.zeros((d, LANES - D_IDX), w_in.dtype)
    cols = [p_conv[:, :c_conv], p_conv[:, c_conv:], qpad, k, v, qi,
            jnp.concatenate([ki, z], axis=1), jnp.concatenate([z, ki], axis=1),
            jnp.concatenate([wi, jnp.zeros((d, LANES - H_IDX), w_in.dtype)], axis=1)]
    w_all = jnp.concatenate(cols, axis=1).astype(BF16)
    assert w_all.shape[1] == _proj_layout(d_mix)[1]
    return w_all


def _own_half(o_pad, n_heads):
    odd = ((jnp.arange(n_heads) // (n_heads // N_KV_HEADS)) % 2 == 1)[:, None]
    return jnp.where(odd, o_pad[..., HEAD_DIM:], o_pad[..., :HEAD_DIM])


def kernel(x_prompt, x_sample, cache_k, cache_v, cache_idx_k, state_conv, page_table, norm_ffn1, ffn1_w_in, ffn1_w_out, norm_mix, w_in, conv_w, conv_b, conv_ln_g, conv_ln_b, w_out, norm_ffn2, ffn2_w_in, ffn2_w_out, norm_final):
    depth = w_in.shape[0]
    bsz, seq, d = x_prompt.shape
    db, dseq, _ = x_sample.shape
    assert dseq == 1
    d_mix = d
    c_conv = d_mix // C_CONV_FRACTION
    n_heads = (d_mix - c_conv) // HEAD_DIM
    page = cache_k.shape[2]
    n_pages = page_table.shape[1]
    past = n_pages * page
    n_sel_p = min(TOPK_MAX, seq // 4)
    n_sel_s = min(TOPK_MAX, (past + dseq) // 4)

    xp = x_prompt.reshape(bsz * seq, d)
    xs = x_sample.reshape(db, d)
    g_final = norm_final.reshape(1, d)
    outs_p, outs_s = [], []

    for l in range(depth):
        f1 = _pack_ffn(ffn1_w_in[l], ffn1_w_out[l])
        f2 = _pack_ffn(ffn2_w_in[l], ffn2_w_out[l])
        w_all = _pack_proj(w_in[l], d_mix)
        conv_p = jnp.stack([conv_b[l], conv_ln_g[l], conv_ln_b[l]])
        wo = w_out[l].astype(BF16)
        woc, woa = wo[:c_conv], wo[c_conv:]
        g1, gm, g2 = norm_ffn1[l].reshape(1, d), norm_mix[l].reshape(1, d), norm_ffn2[l].reshape(1, d)
        last = g_final if l == depth - 1 else None

        xp = _ffn(xp, g1, *f1, name="ffn1_prompt")
        (conv_o, qpad, k_new, v_new, kb, vb, qi, kia, kib, ki_new, wi, u_tail) = _proj(
            xp, gm, w_all, conv_w[l], conv_p, seq_len=seq, name="proj_prompt")
        r3 = lambda a: a.reshape(bsz, seq, a.shape[-1])
        attn = _prompt_attention(r3(qi), r3(wi), r3(kia), r3(kib), r3(qpad), r3(kb), r3(vb), n_sel=n_sel_p)
        xp = _ffn(xp, g2, *f2, mix=(conv_o, attn.reshape(bsz * seq, -1), woc, woa), g_final=last,
                  name="ffn2_prompt")
        outs_p.append((k_new.reshape(bsz, seq, N_KV_HEADS, HEAD_DIM), v_new.reshape(bsz, seq, N_KV_HEADS, HEAD_DIM),
                       ki_new.reshape(bsz, seq, D_IDX), u_tail[:, CONV_HALO - (CONV_WIDTH - 1):, :]))

        state = state_conv[l].astype(F32)
        xs = _ffn(xs, g1, *f1, name="ffn1_sample")
        (conv_o, qpad, k_new, v_new, _, _, qi, _, _, ki_new, wi, u_new) = _proj(
            xs, gm, w_all, conv_w[l], conv_p, seq_len=1, state=jnp.swapaxes(state, 0, 1), name="proj_sample")
        mask = _sample_select(page_table, qi.reshape(db, H_IDX, D_IDX), wi.reshape(db, H_IDX, 1), ki_new,
                              cache_idx_k, layer=l, n_sel=n_sel_s)
        attn = _sample_attend(page_table, mask, _own_half(qpad.reshape(db, n_heads, LANES), n_heads), k_new, v_new,
                              cache_k, cache_v, layer=l).astype(BF16)
        xs = _ffn(xs, g2, *f2, mix=(conv_o, attn, woc, woa), g_final=last, name="ffn2_sample")
        outs_s.append((k_new.reshape(db, 1, N_KV_HEADS, HEAD_DIM), v_new.reshape(db, 1, N_KV_HEADS, HEAD_DIM),
                       ki_new.reshape(db, 1, D_IDX),
                       jnp.concatenate([state[:, 1:, :], u_new[:, None, :]], axis=1)))

    stack = lambda outs, i: jnp.stack([o[i] for o in outs])
    return (xp.reshape(bsz, seq, d), xs.reshape(db, 1, d),
            stack(outs_p, 0), stack(outs_p, 1), stack(outs_p, 2), stack(outs_p, 3),
            stack(outs_s, 0), stack(outs_s, 1), stack(outs_s, 2), stack(outs_s, 3))
```

```python
import functools

import jax
import jax.numpy as jnp
from jax import lax
from jax.experimental import pallas as pl
from jax.experimental.pallas import tpu as pltpu

F32 = jnp.float32
BF16 = jnp.bfloat16
I32 = jnp.int32

C_CONV_FRACTION = 2
CONV_WIDTH = 31
HEAD_DIM = 64
N_KV_HEADS = 4
H_IDX = 16
D_IDX = 64
TOPK_MAX = 256
EPS = 1e-6
ATTN_SCALE = HEAD_DIM ** -0.5
IDX_SCALE = (D_IDX ** -0.5) * (H_IDX ** -0.5)

LANES = 128
SUBLANES = 8
VMEM_LIMIT_BYTES = 56 * 1024 * 1024

FFN_TM = 512
FFN_TF = 512
PROJ_TM = 256
CONV_HALO = 32
ATT_TQ = 128
ATT_TH = 128
ATT_CK_SCORE = 256
ATT_CK = 512
ATT_UNROLL = 4
SEARCH_CAP = 40
PAGE_UNROLL = 3

INT_MIN = -2 ** 31
NEG_BIG = -1e30
F32_LOWEST = float(jnp.finfo(jnp.float32).min)


def _round_up(x, m):
    return (x + m - 1) // m * m


def _cparams(sem):
    return pltpu.CompilerParams(dimension_semantics=sem, vmem_limit_bytes=VMEM_LIMIT_BYTES)


def _resident(shape, index_map):
    return pl.BlockSpec(shape, index_map, pipeline_mode=pl.Buffered(1))


def _rms(x, g):
    ms = jnp.mean(x * x, axis=-1, keepdims=True)
    return x * lax.rsqrt(ms + EPS) * g


def _dot(a, b):
    return jnp.dot(a, b, preferred_element_type=F32)


def _dot_nt(a, b):
    return lax.dot_general(a, b, (((1,), (1,)), ((), ())), preferred_element_type=F32)


def _ffn_kernel(*refs, has_mix, has_final):
    it = iter(refs)
    x_ref = next(it)
    if has_mix:
        mc_ref, ma_ref, woc_ref, woa_ref = next(it), next(it), next(it), next(it)
    g_ref, wa_ref, wb_ref, wo_ref = next(it), next(it), next(it), next(it)
    if has_final:
        gf_ref = next(it)
    o_ref = next(it)
    xn_ref = next(it)

    f = pl.program_id(1)

    @pl.when(f == 0)
    def _():
        x = x_ref[...]
        if has_mix:
            x = x + _dot(mc_ref[...], woc_ref[...]) + _dot(ma_ref[...], woa_ref[...])
        o_ref[...] = x
        xn_ref[...] = _rms(x, g_ref[...]).astype(BF16)

    xn = xn_ref[...]
    a = _dot(xn, wa_ref[...])
    b = _dot(xn, wb_ref[...])
    act = (a * jax.nn.sigmoid(a) * b).astype(BF16)
    o_ref[...] += 0.5 * _dot(act, wo_ref[...])

    if has_final:
        @pl.when(f == pl.num_programs(1) - 1)
        def _():
            o_ref[...] = _rms(o_ref[...], gf_ref[...])


def _ffn(x, g, wa, wb, wo, *, mix=None, g_final=None, name):
    m, d = x.shape
    fp = wa.shape[1]
    tm = min(FFN_TM, m)
    assert m % tm == 0 and fp % FFN_TF == 0
    nf = fp // FFN_TF
    has_mix = mix is not None
    has_final = g_final is not None

    row = lambda i, f: (i, 0)
    const = lambda i, f: (0, 0)
    args = [x]
    specs = [pl.BlockSpec((tm, d), row)]
    if has_mix:
        mc, ma, woc, woa = mix
        args += [mc, ma, woc, woa]
        specs += [pl.BlockSpec((tm, mc.shape[1]), row), pl.BlockSpec((tm, ma.shape[1]), row),
                  _resident(woc.shape, const), _resident(woa.shape, const)]
    args += [g, wa, wb, wo]
    specs += [_resident((1, d), const),
              pl.BlockSpec((d, FFN_TF), lambda i, f: (0, f)),
              pl.BlockSpec((d, FFN_TF), lambda i, f: (0, f)),
              pl.BlockSpec((FFN_TF, d), lambda i, f: (f, 0))]
    if has_final:
        args.append(g_final)
        specs.append(_resident((1, d), const))

    return pl.pallas_call(
        functools.partial(_ffn_kernel, has_mix=has_mix, has_final=has_final),
        grid=(m // tm, nf),
        in_specs=specs,
        out_specs=pl.BlockSpec((tm, d), row),
        out_shape=jax.ShapeDtypeStruct((m, d), F32),
        scratch_shapes=[pltpu.VMEM((tm, d), BF16)],
        compiler_params=_cparams(("arbitrary", "arbitrary")),
        name=name,
    )(*args)


def _proj_layout(d_mix):
    c_conv = d_mix // C_CONV_FRACTION
    d_attn = d_mix - c_conv
    n_heads = d_attn // HEAD_DIM
    widths = dict(ca=c_conv, cg=c_conv, qpad=n_heads * LANES, k=N_KV_HEADS * HEAD_DIM,
                  v=N_KV_HEADS * HEAD_DIM, qi=H_IDX * D_IDX, kia=LANES, kib=LANES, wi=LANES)
    off, o = {}, 0
    for name, w in widths.items():
        off[name] = (o, o + w)
        o += w
    return off, o


def _proj_kernel(*refs, tm, tiles_per_seq, sample, d_mix):
    it = iter(refs)
    x_ref, g_ref, w_ref, cw_ref, cp_ref = next(it), next(it), next(it), next(it), next(it)
    if sample:
        st_ref = next(it)
    (conv_ref, qpad_ref, k_ref, v_ref, kb_ref, vb_ref, qi_ref, kia_ref, kib_ref, ki_ref, wi_ref,
     u_ref) = (next(it) for _ in range(12))
    if not sample:
        win_ref, y_ref, z_ref = next(it), next(it), next(it)

    off, _ = _proj_layout(d_mix)
    c_conv = d_mix // C_CONV_FRACTION

    def cols(name):
        lo, hi = off[name]
        return w_ref[:, lo:hi]

    if not sample:
        @pl.when(pl.program_id(0) % tiles_per_seq == 0)
        def _():
            win_ref[0:CONV_HALO, :] = jnp.zeros((CONV_HALO, c_conv), F32)

    xn = _rms(x_ref[...], g_ref[...]).astype(BF16)
    u = _dot(xn, cols("ca")) * jax.nn.sigmoid(_dot(xn, cols("cg")))

    qpad_ref[...] = (_dot(xn, cols("qpad")) * ATTN_SCALE).astype(BF16)
    kk = _dot(xn, cols("k"))
    vv = _dot(xn, cols("v"))
    k_ref[...] = kk
    v_ref[...] = vv
    kb_ref[...] = kk.astype(BF16)
    ones = jnp.ones((vv.shape[0], LANES - HEAD_DIM), F32)
    vb_ref[...] = jnp.concatenate(
        [t for n in range(N_KV_HEADS) for t in (vv[:, n * HEAD_DIM:(n + 1) * HEAD_DIM], ones)], axis=1).astype(BF16)
    qi_ref[...] = _dot(xn, cols("qi")).astype(BF16)
    kia = _dot(xn, cols("kia"))
    kia_ref[...] = kia.astype(BF16)
    kib_ref[...] = _dot(xn, cols("kib")).astype(BF16)
    ki_ref[...] = kia[:, :D_IDX]
    wi_ref[...] = _dot(xn, cols("wi"))[:, :H_IDX] * IDX_SCALE

    bias = cp_ref[0:1, :]
    ln_g = cp_ref[1:2, :]
    ln_b = cp_ref[2:3, :]

    if sample:
        u_ref[...] = u
        y = bias + cw_ref[CONV_WIDTH - 1:CONV_WIDTH, :] * u
        for j in range(CONV_WIDTH - 1):
            y = y + cw_ref[j:j + 1, :] * st_ref[j]
    else:
        win_ref[CONV_HALO:CONV_HALO + tm, :] = u
        first = CONV_HALO - (CONV_WIDTH - 1)
        for c in range(c_conv // LANES):
            cs = slice(c * LANES, (c + 1) * LANES)
            acc = jnp.zeros((tm, LANES), F32) + bias[:, cs]
            for r in range(SUBLANES):
                taps = [j for j in range(CONV_WIDTH) if (first + j) % SUBLANES == r]
                if taps:
                    base = first + taps[0]
                    span = first + taps[-1] + tm - base
                    if r:
                        z_ref[0:span, :] = win_ref[base:base + span, cs]
                    for j in taps:
                        lo = first + j - base
                        z = z_ref[lo:lo + tm, :] if r else win_ref[first + j:first + j + tm, cs]
                        acc = acc + cw_ref[j:j + 1, cs] * z
            y_ref[:, cs] = acc
        y = y_ref[...]
        tail = win_ref[tm:tm + CONV_HALO, :]
        u_ref[0] = tail
        win_ref[0:CONV_HALO, :] = tail

    mu = jnp.mean(y, axis=-1, keepdims=True)
    var = jnp.mean(jnp.square(y - mu), axis=-1, keepdims=True)
    yn = (y - mu) * lax.rsqrt(var + EPS) * ln_g + ln_b
    conv_ref[...] = (yn * jax.nn.sigmoid(yn)).astype(BF16)


def _proj(x, g, w_all, conv_w, conv_p, *, seq_len, state=None, name):
    m, d = x.shape
    d_mix = d
    c_conv = d_mix // C_CONV_FRACTION
    n_heads = (d_mix - c_conv) // HEAD_DIM
    kvw = N_KV_HEADS * HEAD_DIM
    sample = state is not None
    tm = m if sample else min(PROJ_TM, seq_len)
    assert m % tm == 0 and seq_len % tm == 0 or sample
    assert tm >= CONV_HALO or sample
    nt = m // tm
    tiles_per_seq = max(seq_len // tm, 1)
    n_seq = m // seq_len

    row = lambda i: (i, 0)
    const = lambda i: (0, 0)
    args = [x, g, w_all, conv_w, conv_p]
    specs = [pl.BlockSpec((tm, d), row), _resident((1, d), const), _resident(w_all.shape, const),
             _resident(conv_w.shape, const), _resident(conv_p.shape, const)]
    if sample:
        args.append(state)
        specs.append(_resident(state.shape, lambda i: (0, 0, 0)))

    def out(width, dtype):
        return jax.ShapeDtypeStruct((m, width), dtype), pl.BlockSpec((tm, width), row)

    outs = [out(c_conv, BF16), out(n_heads * LANES, BF16), out(kvw, F32), out(kvw, F32),
            out(kvw, BF16), out(N_KV_HEADS * LANES, BF16), out(H_IDX * D_IDX, BF16), out(LANES, BF16),
            out(LANES, BF16), out(D_IDX, F32), out(H_IDX, F32)]
    if sample:
        outs.append(out(c_conv, F32))
        scratch = []
    else:
        outs.append((jax.ShapeDtypeStruct((n_seq, CONV_HALO, c_conv), F32),
                     pl.BlockSpec((1, CONV_HALO, c_conv), lambda i: (i // tiles_per_seq, 0, 0))))
        scratch = [pltpu.VMEM((tm + CONV_HALO, c_conv), F32), pltpu.VMEM((tm, c_conv), F32),
                   pltpu.VMEM((tm + CONV_HALO, LANES), F32)]

    return pl.pallas_call(
        functools.partial(_proj_kernel, tm=tm, tiles_per_seq=tiles_per_seq, sample=sample, d_mix=d_mix),
        grid=(nt,),
        in_specs=specs,
        out_specs=[o[1] for o in outs],
        out_shape=[o[0] for o in outs],
        scratch_shapes=scratch,
        compiler_params=_cparams(("arbitrary",)),
        name=name,
    )(*args)


def _key_to_f32(key):
    bits = key ^ ((key >> 31) & jnp.int32(0x7FFFFFFF))
    return lax.bitcast_convert_type(bits, F32)


def _chunk_loop(n, body, carry, unroll=2):
    assert unroll & (unroll - 1) == 0

    def run(first, count, carry):
        for u in range(count):
            carry = body(first + u, carry)
        return carry

    carry = lax.fori_loop(0, n // unroll, lambda i, c: run(i * unroll, unroll, c), carry)
    done = n // unroll * unroll
    part = unroll // 2
    while part >= 1:
        carry = lax.cond((n - done) // part % 2 == 1, lambda c, d=done, p=part: run(d, p, c), lambda c: c, carry)
        done = done + jnp.where((n - done) // part % 2 == 1, part, 0)
        part //= 2
    return carry


def _count(sc_ref, nchunks, ck, pred, whole=False):
    rows = sc_ref.shape[0]

    def body(c, acc):
        start = pl.multiple_of(c * ck, ck)
        hit = jnp.where(pred(sc_ref[:, pl.ds(start, ck)], start), 1.0, 0.0)
        part = hit[:, 0:LANES]
        for i in range(1, ck // LANES):
            part = part + hit[:, i * LANES:(i + 1) * LANES]
        return acc + part

    acc = lax.fori_loop(0, nchunks, body, jnp.zeros((rows, LANES), F32))
    per_row = jnp.sum(acc, axis=1, keepdims=True)
    return jnp.sum(per_row, axis=0, keepdims=True) if whole else per_row


def _position(shape, start, whole):
    col = start + lax.broadcasted_iota(I32, shape, 1)
    return lax.broadcasted_iota(I32, shape, 0) * shape[1] + col if whole else col


def _exact_threshold(sc_ref, nchunks, ck, n_sel, n_adm, total_cols, whole=False):
    rows = 1 if whole else sc_ref.shape[0]
    want = jnp.float32(n_sel)

    def bisect(i, carry):
        t, ct = carry
        cand = t + lax.shift_left(jnp.int32(1), 31 - i)
        thr = _key_to_f32(cand)
        cnt = _count(sc_ref, nchunks, ck, lambda blk, _: blk >= thr, whole)
        take = cnt >= want
        return jnp.where(take, cand, t), jnp.where(take, cnt, ct)

    t, ct = lax.fori_loop(0, 32, bisect, (jnp.full((rows, 1), INT_MIN, I32), jnp.zeros((rows, 1), F32)))
    full = n_adm <= n_sel
    thr = jnp.where(full, F32_LOWEST, _key_to_f32(t))
    tied = jnp.logical_and(jnp.logical_not(full), ct > want)

    @pl.when(jnp.sum(jnp.where(tied, 1.0, 0.0)) > 0.0)
    def _():
        n_gt = _count(sc_ref, nchunks, ck, lambda blk, _: blk > thr, whole)
        room = want - n_gt
        nbits = max(int(total_cols - 1).bit_length(), 1)

        def search(i, q):
            cand = q + lax.shift_left(jnp.int32(1), nbits - 1 - i)
            below = _count(sc_ref, nchunks, ck,
                           lambda blk, s: jnp.logical_and(blk == thr, _position(blk.shape, s, whole) < cand), whole)
            return jnp.where(below < room, cand, q)

        last = lax.fori_loop(0, nbits, search, jnp.zeros((rows, 1), I32))

        def drop(c, _):
            start = pl.multiple_of(c * ck, ck)
            blk = sc_ref[:, pl.ds(start, ck)]
            lose = jnp.logical_and(tied, jnp.logical_and(blk == thr, _position(blk.shape, start, whole) > last))
            sc_ref[:, pl.ds(start, ck)] = jnp.where(lose, -jnp.inf, blk)
            return 0

        lax.fori_loop(0, nchunks, drop, 0)

    return thr


def _select_threshold(sc_ref, thr_ref, nchunks, ck, n_sel, n_adm, total_cols, row_lo, row_hi, whole=False):
    rows = 1 if whole else sc_ref.shape[0]
    want = jnp.float32(n_sel)
    full = n_adm <= n_sel

    def unresolved(done):
        return jnp.sum(done) < rows

    def cond(state):
        it, go = state[0], state[1]
        return jnp.logical_and(it < SEARCH_CAP, go)

    def body(state):
        it, _, lo, hi, thr, done = state
        mid = lo + (hi - lo) * 0.5
        cnt = _count(sc_ref, nchunks, ck, lambda blk, _: blk >= mid, whole)
        go = unresolved(done)
        found = cnt == want
        thr = jnp.where(jnp.logical_and(found, done == 0.0), mid, thr)
        done = jnp.where(found, 1.0, done)
        return it + 1, go, jnp.where(cnt > want, mid, lo), jnp.where(cnt < want, mid, hi), thr, done

    done0 = jnp.where(full, 1.0, 0.0)
    init = (jnp.int32(0), unresolved(done0), row_lo, row_hi, jnp.where(full, F32_LOWEST, row_lo), done0)
    _, _, _, _, thr, done = lax.while_loop(cond, body, init)
    thr_ref[...] = thr

    @pl.when(unresolved(done))
    def _():
        exact = _exact_threshold(sc_ref, nchunks, ck, n_sel, n_adm, total_cols, whole)
        thr_ref[...] = jnp.where(done > 0.0, thr, exact)

    return thr_ref[...]


def _attn_kernel(qi_ref, w_ref, kia_ref, kib_ref, q_ref, k_ref, v_ref, o_ref, sc_ref, thr_ref, s_ref, wb_ref,
                 *, tq, th, n_sel, seq_len):
    t0 = pl.program_id(1) * tq
    nk = t0 // ATT_CK + 1
    row_pos = t0 + lax.broadcasted_iota(I32, (tq, 1), 0)
    n_heads = q_ref.shape[2] // LANES
    group = n_heads // N_KV_HEADS

    def fold(op, acc, x):
        for i in range(x.shape[1] // LANES):
            acc = op(acc, x[:, i * LANES:(i + 1) * LANES])
        return acc

    for h in range(H_IDX):
        wb_ref[h] = jnp.broadcast_to(w_ref[0, :, h:h + 1], (tq, LANES))

    def head_weight(h):
        return jnp.concatenate([wb_ref[h]] * (ATT_CK_SCORE // LANES), axis=1)

    def score(c, carry):
        hi_run, lo_run = carry
        start = pl.multiple_of(c * ATT_CK_SCORE, ATT_CK_SCORE)
        ka = kia_ref[0, pl.ds(start, ATT_CK_SCORE), :]
        kb = kib_ref[0, pl.ds(start, ATT_CK_SCORE), :]
        acc = jnp.zeros((tq, ATT_CK_SCORE), F32)
        for j in range(H_IDX // 2):
            pair = qi_ref[0, :, j * LANES:(j + 1) * LANES]
            acc = acc + jnp.maximum(_dot_nt(pair, ka), 0.0) * head_weight(2 * j)
            acc = acc + jnp.maximum(_dot_nt(pair, kb), 0.0) * head_weight(2 * j + 1)
        admissible = start + lax.broadcasted_iota(I32, (tq, ATT_CK_SCORE), 1) <= row_pos
        masked = jnp.where(admissible, acc, -jnp.inf)
        sc_ref[:, pl.ds(start, ATT_CK_SCORE)] = masked
        return fold(jnp.maximum, hi_run, masked), fold(jnp.minimum, lo_run, jnp.where(admissible, acc, jnp.inf))

    hi_run, lo_run = _chunk_loop(nk * (ATT_CK // ATT_CK_SCORE), score,
                                 (jnp.full((tq, LANES), -jnp.inf, F32), jnp.full((tq, LANES), jnp.inf, F32)))

    thr = _select_threshold(sc_ref, thr_ref, nk, ATT_CK, n_sel, row_pos + 1, seq_len,
                            jnp.min(lo_run, axis=1, keepdims=True), jnp.max(hi_run, axis=1, keepdims=True))

    def to_mask(c, _):
        start = pl.multiple_of(c * ATT_CK, ATT_CK)
        sc_ref[:, pl.ds(start, ATT_CK)] = jnp.where(sc_ref[:, pl.ds(start, ATT_CK)] >= thr, 0.0, NEG_BIG)
        return 0

    lax.fori_loop(0, nk, to_mask, 0)

    rows = group * th

    def attend(u, _):
        r0 = pl.multiple_of(u // N_KV_HEADS * th, th)
        n = u % N_KV_HEADS
        kv_lanes = pl.ds(pl.multiple_of(n // 2 * LANES, LANES), LANES)
        qg = jnp.concatenate(
            [q_ref[0, pl.ds(r0, th), pl.ds(pl.multiple_of((group * n + g) * LANES, LANES), LANES)]
             for g in range(group)], axis=0)

        def logits(c, m_run):
            start = pl.multiple_of(c * ATT_CK, ATT_CK)
            s = (_dot_nt(qg, k_ref[0, pl.ds(start, ATT_CK), kv_lanes])
                 + jnp.concatenate([sc_ref[pl.ds(r0, th), pl.ds(start, ATT_CK)]] * group, axis=0))
            s_ref[:, pl.ds(start, ATT_CK)] = s
            return fold(jnp.maximum, m_run, s)

        m_run = _chunk_loop(nk, logits, jnp.full((rows, LANES), -jnp.inf, F32), unroll=ATT_UNROLL)
        m = jnp.max(m_run, axis=1, keepdims=True)

        v_lanes = pl.ds(pl.multiple_of(n * LANES, LANES), LANES)

        def weigh(c, acc):
            start = pl.multiple_of(c * ATT_CK, ATT_CK)
            p = jnp.exp(s_ref[:, pl.ds(start, ATT_CK)] - m)
            return acc + _dot(p.astype(BF16), v_ref[0, pl.ds(start, ATT_CK), v_lanes])

        acc = _chunk_loop(nk, weigh, jnp.zeros((rows, LANES), F32), unroll=ATT_UNROLL)
        out = acc[:, :HEAD_DIM] / acc[:, HEAD_DIM:]
        width = group * HEAD_DIM
        o_ref[0, pl.ds(r0, th), pl.ds(pl.multiple_of(n * width, width), width)] = jnp.concatenate(
            [out[g * th:(g + 1) * th] for g in range(group)], axis=1).astype(o_ref.dtype)
        return 0

    lax.fori_loop(0, tq // th * N_KV_HEADS, attend, 0)


def _prompt_attention(qi, wi, kia, kib, qpad, kb, vb, *, n_sel):
    b, s, _ = qi.shape
    tq = min(ATT_TQ, s)
    th = min(ATT_TH, tq)
    assert s % ATT_CK == 0 and ATT_CK % tq == 0 and ATT_CK % ATT_CK_SCORE == 0 and tq % th == 0
    n_heads = qpad.shape[2] // LANES
    tile = lambda bi, i: (bi, i, 0)
    seq = lambda bi, i: (bi, 0, 0)
    return pl.pallas_call(
        functools.partial(_attn_kernel, tq=tq, th=th, n_sel=n_sel, seq_len=s),
        grid=(b, s // tq),
        in_specs=[pl.BlockSpec((1, tq, qi.shape[2]), tile), pl.BlockSpec((1, tq, wi.shape[2]), tile),
                  _resident((1, s, LANES), seq), _resident((1, s, LANES), seq),
                  pl.BlockSpec((1, tq, qpad.shape[2]), tile),
                  _resident((1, s, kb.shape[2]), seq), _resident((1, s, vb.shape[2]), seq)],
        out_specs=pl.BlockSpec((1, tq, n_heads * HEAD_DIM), tile),
        out_shape=jax.ShapeDtypeStruct((b, s, n_heads * HEAD_DIM), BF16),
        scratch_shapes=[pltpu.VMEM((tq, s), F32), pltpu.VMEM((tq, 1), F32),
                        pltpu.VMEM((n_heads // N_KV_HEADS * th, s), F32), pltpu.VMEM((H_IDX, tq, LANES), F32)],
        compiler_params=_cparams(("arbitrary", "arbitrary")),
        name="prompt_attention",
    )(qi, wi, kia, kib, qpad, kb, vb)


def _native_pages(cache):
    rank = cache.ndim
    t = jnp.transpose(cache, (0, 1) + tuple(range(3, rank)) + (2,))
    return t.reshape(-1, cache.shape[2])


def _slot0_page(x, page):
    return jnp.pad(x[:, :, None], ((0, 0), (0, 0), (0, page - 1)))


def _sample_select_kernel(pt_ref, qi_ref, w_ref, kin_ref, cache_ref, mask_ref, buf_ref, row_ref, sc_ref, thr_ref, sem,
                          *, layer, n_pool, n_pages, page, n_sel, rows):
    b = pl.program_id(0)
    past = n_pages * page
    total = rows * LANES

    def page_copy(j):
        src = pl.ds((layer * n_pool + pt_ref[b, j]) * D_IDX, D_IDX)
        return pltpu.make_async_copy(cache_ref.at[src], buf_ref.at[:, pl.ds(j * page, page)], sem)

    def start(j, _):
        page_copy(j).start()
        return 0

    def wait(j, _):
        page_copy(j).wait()
        return 0

    lax.fori_loop(0, n_pages, start, 0)

    @pl.when(b == 0)
    def _():
        buf_ref[:, past + page:] = jnp.zeros((D_IDX, total - past - page), F32)

    buf_ref[:, past:past + page] = kin_ref[0]
    lax.fori_loop(0, n_pages, wait, 0)

    x = _dot(qi_ref[0], buf_ref[...].astype(BF16))
    score = jnp.sum(jnp.maximum(x, 0.0) * w_ref[0], axis=0, keepdims=True)
    admissible = lax.broadcasted_iota(I32, (1, total), 1) <= past
    masked = jnp.where(admissible, score, -jnp.inf)
    row_ref[...] = masked
    for r in range(rows):
        sc_ref[r:r + 1, :] = row_ref[:, r * LANES:(r + 1) * LANES]
    thr = _select_threshold(sc_ref, thr_ref, 1, LANES, n_sel, jnp.full((1, 1), past + 1, I32), total,
                            jnp.min(jnp.where(admissible, score, jnp.inf), axis=1, keepdims=True),
                            jnp.max(masked, axis=1, keepdims=True), whole=True)
    mask_ref[0] = jnp.where(sc_ref[...] >= thr, 0.0, NEG_BIG)


def _sample_select(page_table, qi, wi, ki_new, cache_idx, *, layer, n_sel):
    db, n_pages = page_table.shape
    depth, n_pool, page, _ = cache_idx.shape
    assert page == LANES
    past = n_pages * page
    total = _round_up(past + page, SUBLANES * LANES)
    rows = total // LANES
    grid_spec = pltpu.PrefetchScalarGridSpec(
        num_scalar_prefetch=1,
        grid=(db,),
        in_specs=[pl.BlockSpec((1, H_IDX, D_IDX), lambda b, pt: (b, 0, 0)),
                  pl.BlockSpec((1, H_IDX, 1), lambda b, pt: (b, 0, 0)),
                  pl.BlockSpec((1, D_IDX, page), lambda b, pt: (b, 0, 0)),
                  pl.BlockSpec(memory_space=pl.ANY)],
        out_specs=pl.BlockSpec((1, rows, LANES), lambda b, pt: (b, 0, 0)),
        scratch_shapes=[pltpu.VMEM((D_IDX, total), F32), pltpu.VMEM((1, total), F32),
                        pltpu.VMEM((rows, LANES), F32), pltpu.VMEM((1, 1), F32), pltpu.SemaphoreType.DMA(())],
    )
    return pl.pallas_call(
        functools.partial(_sample_select_kernel, layer=layer, n_pool=n_pool, n_pages=n_pages, page=page, n_sel=n_sel,
                          rows=rows),
        grid_spec=grid_spec,
        out_shape=jax.ShapeDtypeStruct((db, rows, LANES), F32),
        compiler_params=_cparams(("arbitrary",)),
        name="sample_select",
    )(page_table, qi, wi, _slot0_page(ki_new, page), _native_pages(cache_idx))


def _sample_attend_kernel(pt_ref, mask_ref, qt_ref, knew_ref, vnew_ref, ck_ref, cv_ref, o_ref,
                          kbuf, vbuf, s_ref, qb_ref, sem_k, sem_v, *, layer, n_pool, n_pages, page, rows):
    b = pl.program_id(0)
    n_heads = qt_ref.shape[2]
    group = n_heads // N_KV_HEADS
    kvw = N_KV_HEADS * HEAD_DIM
    live = n_pages + 1

    def fetch(cache_ref, new_ref, buf, sem, sample, wait):
        def one(src, j):
            cp = pltpu.make_async_copy(src, buf.at[pl.ds(j * kvw, kvw)], sem)
            cp.wait() if wait else cp.start()

        def cached(j, _):
            one(cache_ref.at[pl.ds((layer * n_pool + pt_ref[sample, j]) * kvw, kvw)], j)
            return 0

        lax.fori_loop(0, n_pages, cached, 0)
        one(new_ref.at[sample], n_pages)

    k_pages = functools.partial(fetch, ck_ref, knew_ref, kbuf, sem_k)
    v_pages = functools.partial(fetch, cv_ref, vnew_ref, vbuf, sem_v)
    more = b + 1 < pl.num_programs(0)

    @pl.when(b == 0)
    def _():
        k_pages(0, wait=False)
        v_pages(0, wait=False)

    for h in range(n_heads):
        qb_ref[h] = jnp.broadcast_to(qt_ref[0, :, h:h + 1], (HEAD_DIM, LANES))

    s_ref[:, live * page:] = jnp.full((n_heads, (rows - live) * page), NEG_BIG, F32)
    k_pages(b, wait=True)

    def logits(j, _):
        row0 = pl.multiple_of(j * kvw, kvw)
        col0 = pl.multiple_of(j * page, page)
        drop = mask_ref[0, pl.ds(j, 1), :]
        for n in range(N_KV_HEADS):
            k_n = kbuf[pl.ds(row0 + n * HEAD_DIM, HEAD_DIM), :]
            for g in range(group):
                h = n * group + g
                s_ref[h:h + 1, pl.ds(col0, page)] = jnp.sum(k_n * qb_ref[h], axis=0, keepdims=True) + drop
        return 0

    lax.fori_loop(0, live, logits, 0, unroll=PAGE_UNROLL)

    @pl.when(more)
    def _():
        k_pages(b + 1, wait=False)

    s = s_ref[...]
    p = jnp.exp(s - jnp.max(s, axis=1, keepdims=True))
    s_ref[...] = p / jnp.sum(p, axis=1, keepdims=True)
    v_pages(b, wait=True)

    for h in range(n_heads):
        def weigh(j, acc):
            rows_h = pl.ds(pl.multiple_of(j * kvw, kvw) + h // group * HEAD_DIM, HEAD_DIM)
            return acc + vbuf[rows_h, :] * s_ref[h:h + 1, pl.ds(pl.multiple_of(j * page, page), page)]

        acc = lax.fori_loop(0, live, weigh, jnp.zeros((HEAD_DIM, LANES), F32), unroll=PAGE_UNROLL)
        o_ref[0, :, h:h + 1] = jnp.sum(acc, axis=1, keepdims=True)

    @pl.when(more)
    def _():
        v_pages(b + 1, wait=False)


def _sample_attend(page_table, mask, q, k_new, v_new, cache_k, cache_v, *, layer):
    db, n_pages = page_table.shape
    depth, n_pool, page = cache_k.shape[:3]
    n_heads = q.shape[1]
    rows = mask.shape[1]
    kvw = N_KV_HEADS * HEAD_DIM
    assert page == LANES and rows > n_pages
    grid_spec = pltpu.PrefetchScalarGridSpec(
        num_scalar_prefetch=1,
        grid=(db,),
        in_specs=[pl.BlockSpec((1, rows, LANES), lambda b, pt: (b, 0, 0)),
                  pl.BlockSpec((1, HEAD_DIM, n_heads), lambda b, pt: (b, 0, 0)),
                  pl.BlockSpec(memory_space=pl.ANY), pl.BlockSpec(memory_space=pl.ANY),
                  pl.BlockSpec(memory_space=pl.ANY), pl.BlockSpec(memory_space=pl.ANY)],
        out_specs=pl.BlockSpec((1, HEAD_DIM, n_heads), lambda b, pt: (b, 0, 0)),
        scratch_shapes=[pltpu.VMEM(((n_pages + 1) * kvw, page), F32), pltpu.VMEM(((n_pages + 1) * kvw, page), F32),
                        pltpu.VMEM((n_heads, rows * LANES), F32), pltpu.VMEM((n_heads, HEAD_DIM, LANES), F32),
                        pltpu.SemaphoreType.DMA(()), pltpu.SemaphoreType.DMA(())],
    )
    o_t = pl.pallas_call(
        functools.partial(_sample_attend_kernel, layer=layer, n_pool=n_pool, n_pages=n_pages, page=page, rows=rows),
        grid_spec=grid_spec,
        out_shape=jax.ShapeDtypeStruct((db, HEAD_DIM, n_heads), F32),
        compiler_params=_cparams(("arbitrary",)),
        name="sample_attend",
    )(page_table, mask, jnp.swapaxes(q, 1, 2).astype(F32), _slot0_page(k_new, page), _slot0_page(v_new, page),
      _native_pages(cache_k), _native_pages(cache_v))
    return jnp.swapaxes(o_t, 1, 2).reshape(db, n_heads * HEAD_DIM)


def _pack_ffn(w_in, w_out):
    d, f2 = w_in.shape
    f = f2 // 2
    fp = _round_up(f, FFN_TF)
    pad = lambda w, axis: jnp.pad(w, [(0, fp - f) if a == axis else (0, 0) for a in range(2)])
    return (pad(w_in[:, :f], 1).astype(BF16), pad(w_in[:, f:], 1).astype(BF16), pad(w_out, 0).astype(BF16))


def _pack_proj(w_in, d_mix):
    d = w_in.shape[0]
    c_conv = d_mix // C_CONV_FRACTION
    d_attn = d_mix - c_conv
    n_heads = d_attn // HEAD_DIM
    kvw = N_KV_HEADS * HEAD_DIM
    sizes = [2 * c_conv, d_attn, kvw, kvw, H_IDX * D_IDX, D_IDX, H_IDX]
    parts, o = [], 0
    for sz in sizes:
        parts.append(w_in[:, o:o + sz])
        o += sz
    p_conv, q, k, v, qi, ki, wi = parts
    qh = q.reshape(d, n_heads, HEAD_DIM)
    zero = jnp.zeros_like(qh)
    odd = ((jnp.arange(n_heads) // (n_heads // N_KV_HEADS)) % 2 == 1)[None, :, None]
    qpad = jnp.concatenate([jnp.where(odd, zero, qh), jnp.where(odd, qh, zero)], axis=-1).reshape(d, n_heads * LANES)
    z = jnp.zeros((d, LANES - D_IDX), w_in.dtype)
    cols = [p_conv[:, :c_conv], p_conv[:, c_conv:], qpad, k, v, qi,
            jnp.concatenate([ki, z], axis=1), jnp.concatenate([z, ki], axis=1),
            jnp.concatenate([wi, jnp.zeros((d, LANES - H_IDX), w_in.dtype)], axis=1)]
    w_all = jnp.concatenate(cols, axis=1).astype(BF16)
    assert w_all.shape[1] == _proj_layout(d_mix)[1]
    return w_all


def _own_half(o_pad, n_heads):
    odd = ((jnp.arange(n_heads) // (n_heads // N_KV_HEADS)) % 2 == 1)[:, None]
    return jnp.where(odd, o_pad[..., HEAD_DIM:], o_pad[..., :HEAD_DIM])


def kernel(x_prompt, x_sample, cache_k, cache_v, cache_idx_k, state_conv, page_table, norm_ffn1, ffn1_w_in, ffn1_w_out, norm_mix, w_in, conv_w, conv_b, conv_ln_g, conv_ln_b, w_out, norm_ffn2, ffn2_w_in, ffn2_w_out, norm_final):
    depth = w_in.shape[0]
    bsz, seq, d = x_prompt.shape
    db, dseq, _ = x_sample.shape
    assert dseq == 1
    d_mix = d
    c_conv = d_mix // C_CONV_FRACTION
    n_heads = (d_mix - c_conv) // HEAD_DIM
    page = cache_k.shape[2]
    n_pages = page_table.shape[1]
    past = n_pages * page
    n_sel_p = min(TOPK_MAX, seq // 4)
    n_sel_s = min(TOPK_MAX, (past + dseq) // 4)

    xp = x_prompt.reshape(bsz * seq, d)
    xs = x_sample.reshape(db, d)
    g_final = norm_final.reshape(1, d)
    outs_p, outs_s = [], []

    for l in range(depth):
        f1 = _pack_ffn(ffn1_w_in[l], ffn1_w_out[l])
        f2 = _pack_ffn(ffn2_w_in[l], ffn2_w_out[l])
        w_all = _pack_proj(w_in[l], d_mix)
        conv_p = jnp.stack([conv_b[l], conv_ln_g[l], conv_ln_b[l]])
        wo = w_out[l].astype(BF16)
        woc, woa = wo[:c_conv], wo[c_conv:]
        g1, gm, g2 = norm_ffn1[l].reshape(1, d), norm_mix[l].reshape(1, d), norm_ffn2[l].reshape(1, d)
        last = g_final if l == depth - 1 else None

        xp = _ffn(xp, g1, *f1, name="ffn1_prompt")
        (conv_o, qpad, k_new, v_new, kb, vb, qi, kia, kib, ki_new, wi, u_tail) = _proj(
            xp, gm, w_all, conv_w[l], conv_p, seq_len=seq, name="proj_prompt")
        r3 = lambda a: a.reshape(bsz, seq, a.shape[-1])
        attn = _prompt_attention(r3(qi), r3(wi), r3(kia), r3(kib), r3(qpad), r3(kb), r3(vb), n_sel=n_sel_p)
        xp = _ffn(xp, g2, *f2, mix=(conv_o, attn.reshape(bsz * seq, -1), woc, woa), g_final=last,
                  name="ffn2_prompt")
        outs_p.append((k_new.reshape(bsz, seq, N_KV_HEADS, HEAD_DIM), v_new.reshape(bsz, seq, N_KV_HEADS, HEAD_DIM),
                       ki_new.reshape(bsz, seq, D_IDX), u_tail[:, CONV_HALO - (CONV_WIDTH - 1):, :]))

        state = state_conv[l].astype(F32)
        xs = _ffn(xs, g1, *f1, name="ffn1_sample")
        (conv_o, qpad, k_new, v_new, _, _, qi, _, _, ki_new, wi, u_new) = _proj(
            xs, gm, w_all, conv_w[l], conv_p, seq_len=1, state=jnp.swapaxes(state, 0, 1), name="proj_sample")
        mask = _sample_select(page_table, qi.reshape(db, H_IDX, D_IDX), wi.reshape(db, H_IDX, 1), ki_new,
                              cache_idx_k, layer=l, n_sel=n_sel_s)
        attn = _sample_attend(page_table, mask, _own_half(qpad.reshape(db, n_heads, LANES), n_heads), k_new, v_new,
                              cache_k, cache_v, layer=l).astype(BF16)
        xs = _ffn(xs, g2, *f2, mix=(conv_o, attn, woc, woa), g_final=last, name="ffn2_sample")
        outs_s.append((k_new.reshape(db, 1, N_KV_HEADS, HEAD_DIM), v_new.reshape(db, 1, N_KV_HEADS, HEAD_DIM),
                       ki_new.reshape(db, 1, D_IDX),
                       jnp.concatenate([state[:, 1:, :], u_new[:, None, :]], axis=1)))

    stack = lambda outs, i: jnp.stack([o[i] for o in outs])
    return (xp.reshape(bsz, seq, d), xs.reshape(db, 1, d),
            stack(outs_p, 0), stack(outs_p, 1), stack(outs_p, 2), stack(outs_p, 3),
            stack(outs_s, 0), stack(outs_s, 1), stack(outs_s, 2), stack(outs_s, 3))
```

```python
import functools

import jax
import jax.numpy as jnp
from jax import lax
from jax.experimental import pallas as pl
from jax.experimental.pallas import tpu as pltpu

F32 = jnp.float32
BF16 = jnp.bfloat16
I32 = jnp.int32

C_CONV_FRACTION = 2
CONV_WIDTH = 31
HEAD_DIM = 64
N_KV_HEADS = 4
H_IDX = 16
D_IDX = 64
TOPK_MAX = 256
EPS = 1e-6
ATTN_SCALE = HEAD_DIM ** -0.5
IDX_SCALE = (D_IDX ** -0.5) * (H_IDX ** -0.5)

LANES = 128
SUBLANES = 8
VMEM_LIMIT_BYTES = 56 * 1024 * 1024

FFN_TM = 512
FFN_TF = 512
PROJ_TM = 256
CONV_HALO = 32
ATT_TQ = 128
ATT_TH = 128
ATT_CK_SCORE = 256
ATT_CK = 512
ATT_UNROLL = 8
SCORE_UNROLL = 4
SEARCH_CAP = 40
PAGE_UNROLL = 3

INT_MIN = -2 ** 31
NEG_BIG = -1e30
F32_LOWEST = float(jnp.finfo(jnp.float32).min)


def _round_up(x, m):
    return (x + m - 1) // m * m


def _cparams(sem):
    return pltpu.CompilerParams(dimension_semantics=sem, vmem_limit_bytes=VMEM_LIMIT_BYTES)


def _resident(shape, index_map):
    return pl.BlockSpec(shape, index_map, pipeline_mode=pl.Buffered(1))


def _rms(x, g):
    ms = jnp.mean(x * x, axis=-1, keepdims=True)
    return x * lax.rsqrt(ms + EPS) * g


def _dot(a, b):
    return jnp.dot(a, b, preferred_element_type=F32)


def _dot_nt(a, b):
    return lax.dot_general(a, b, (((1,), (1,)), ((), ())), preferred_element_type=F32)


def _ffn_kernel(*refs, has_mix, has_final):
    it = iter(refs)
    x_ref = next(it)
    if has_mix:
        mc_ref, ma_ref, woc_ref, woa_ref = next(it), next(it), next(it), next(it)
    g_ref, wa_ref, wb_ref, wo_ref = next(it), next(it), next(it), next(it)
    if has_final:
        gf_ref = next(it)
    o_ref = next(it)
    xn_ref = next(it)

    f = pl.program_id(1)

    @pl.when(f == 0)
    def _():
        x = x_ref[...]
        if has_mix:
            x = x + _dot(mc_ref[...], woc_ref[...]) + _dot(ma_ref[...], woa_ref[...])
        o_ref[...] = x
        xn_ref[...] = _rms(x, g_ref[...]).astype(BF16)

    xn = xn_ref[...]
    a = _dot(xn, wa_ref[...])
    b = _dot(xn, wb_ref[...])
    act = (a * jax.nn.sigmoid(a) * b).astype(BF16)
    o_ref[...] += 0.5 * _dot(act, wo_ref[...])

    if has_final:
        @pl.when(f == pl.num_programs(1) - 1)
        def _():
            o_ref[...] = _rms(o_ref[...], gf_ref[...])


def _ffn(x, g, wa, wb, wo, *, mix=None, g_final=None, name):
    m, d = x.shape
    fp = wa.shape[1]
    tm = min(FFN_TM, m)
    assert m % tm == 0 and fp % FFN_TF == 0
    nf = fp // FFN_TF
    has_mix = mix is not None
    has_final = g_final is not None

    row = lambda i, f: (i, 0)
    const = lambda i, f: (0, 0)
    args = [x]
    specs = [pl.BlockSpec((tm, d), row)]
    if has_mix:
        mc, ma, woc, woa = mix
        args += [mc, ma, woc, woa]
        specs += [pl.BlockSpec((tm, mc.shape[1]), row), pl.BlockSpec((tm, ma.shape[1]), row),
                  _resident(woc.shape, const), _resident(woa.shape, const)]
    args += [g, wa, wb, wo]
    specs += [_resident((1, d), const),
              pl.BlockSpec((d, FFN_TF), lambda i, f: (0, f)),
              pl.BlockSpec((d, FFN_TF), lambda i, f: (0, f)),
              pl.BlockSpec((FFN_TF, d), lambda i, f: (f, 0))]
    if has_final:
        args.append(g_final)
        specs.append(_resident((1, d), const))

    return pl.pallas_call(
        functools.partial(_ffn_kernel, has_mix=has_mix, has_final=has_final),
        grid=(m // tm, nf),
        in_specs=specs,
        out_specs=pl.BlockSpec((tm, d), row),
        out_shape=jax.ShapeDtypeStruct((m, d), F32),
        scratch_shapes=[pltpu.VMEM((tm, d), BF16)],
        compiler_params=_cparams(("arbitrary", "arbitrary")),
        name=name,
    )(*args)


def _proj_layout(d_mix):
    c_conv = d_mix // C_CONV_FRACTION
    d_attn = d_mix - c_conv
    n_heads = d_attn // HEAD_DIM
    widths = dict(ca=c_conv, cg=c_conv, qpad=n_heads * LANES, k=N_KV_HEADS * HEAD_DIM,
                  v=N_KV_HEADS * HEAD_DIM, qi=H_IDX * D_IDX, kia=LANES, kib=LANES, wi=LANES)
    off, o = {}, 0
    for name, w in widths.items():
        off[name] = (o, o + w)
        o += w
    return off, o


def _proj_kernel(*refs, tm, tiles_per_seq, sample, d_mix):
    it = iter(refs)
    x_ref, g_ref, w_ref, cw_ref, cp_ref = next(it), next(it), next(it), next(it), next(it)
    if sample:
        st_ref = next(it)
    (conv_ref, qpad_ref, k_ref, v_ref, kb_ref, vb_ref, qi_ref, kia_ref, kib_ref, ki_ref, wi_ref,
     u_ref) = (next(it) for _ in range(12))
    if not sample:
        win_ref, y_ref, z_ref = next(it), next(it), next(it)

    off, _ = _proj_layout(d_mix)
    c_conv = d_mix // C_CONV_FRACTION

    def cols(name):
        lo, hi = off[name]
        return w_ref[:, lo:hi]

    if not sample:
        @pl.when(pl.program_id(0) % tiles_per_seq == 0)
        def _():
            win_ref[0:CONV_HALO, :] = jnp.zeros((CONV_HALO, c_conv), F32)

    xn = _rms(x_ref[...], g_ref[...]).astype(BF16)
    u = _dot(xn, cols("ca")) * jax.nn.sigmoid(_dot(xn, cols("cg")))

    bias = cp_ref[0:1, :]
    ln_g = cp_ref[1:2, :]
    ln_b = cp_ref[2:3, :]

    if sample:
        u_ref[...] = u
        y = bias + cw_ref[CONV_WIDTH - 1:CONV_WIDTH, :] * u
        for j in range(CONV_WIDTH - 1):
            y = y + cw_ref[j:j + 1, :] * st_ref[j]
    else:
        win_ref[CONV_HALO:CONV_HALO + tm, :] = u
        first = CONV_HALO - (CONV_WIDTH - 1)
        for c in range(c_conv // LANES):
            cs = slice(c * LANES, (c + 1) * LANES)
            acc = jnp.zeros((tm, LANES), F32) + bias[:, cs]
            for r in range(SUBLANES):
                taps = [j for j in range(CONV_WIDTH) if (first + j) % SUBLANES == r]
                if taps:
                    base = first + taps[0]
                    span = first + taps[-1] + tm - base
                    if r:
                        z_ref[0:span, :] = win_ref[base:base + span, cs]
                    for j in taps:
                        lo = first + j - base
                        z = z_ref[lo:lo + tm, :] if r else win_ref[first + j:first + j + tm, cs]
                        acc = acc + cw_ref[j:j + 1, cs] * z
            y_ref[:, cs] = acc
        y = y_ref[...]
        tail = win_ref[tm:tm + CONV_HALO, :]
        u_ref[0] = tail
        win_ref[0:CONV_HALO, :] = tail

    mu = jnp.mean(y, axis=-1, keepdims=True)
    var = jnp.mean(jnp.square(y - mu), axis=-1, keepdims=True)
    yn = (y - mu) * lax.rsqrt(var + EPS) * ln_g + ln_b
    conv_ref[...] = (yn * jax.nn.sigmoid(yn)).astype(BF16)

    qpad_ref[...] = (_dot(xn, cols("qpad")) * ATTN_SCALE).astype(BF16)
    kk = _dot(xn, cols("k"))
    vv = _dot(xn, cols("v"))
    k_ref[...] = kk
    v_ref[...] = vv
    kb_ref[...] = kk.astype(BF16)
    ones = jnp.ones((vv.shape[0], LANES - HEAD_DIM), F32)
    vb_ref[...] = jnp.concatenate(
        [t for n in range(N_KV_HEADS) for t in (vv[:, n * HEAD_DIM:(n + 1) * HEAD_DIM], ones)], axis=1).astype(BF16)
    qi_ref[...] = _dot(xn, cols("qi")).astype(BF16)
    kia = _dot(xn, cols("kia"))
    kia_ref[...] = kia.astype(BF16)
    kib_ref[...] = _dot(xn, cols("kib")).astype(BF16)
    ki_ref[...] = kia[:, :D_IDX]
    wi_ref[...] = _dot(xn, cols("wi"))[:, :H_IDX] * IDX_SCALE


def _proj(x, g, w_all, conv_w, conv_p, *, seq_len, state=None, name):
    m, d = x.shape
    d_mix = d
    c_conv = d_mix // C_CONV_FRACTION
    n_heads = (d_mix - c_conv) // HEAD_DIM
    kvw = N_KV_HEADS * HEAD_DIM
    sample = state is not None
    tm = m if sample else min(PROJ_TM, seq_len)
    assert m % tm == 0 and seq_len % tm == 0 or sample
    assert tm >= CONV_HALO or sample
    nt = m // tm
    tiles_per_seq = max(seq_len // tm, 1)
    n_seq = m // seq_len

    row = lambda i: (i, 0)
    const = lambda i: (0, 0)
    args = [x, g, w_all, conv_w, conv_p]
    specs = [pl.BlockSpec((tm, d), row), _resident((1, d), const), _resident(w_all.shape, const),
             _resident(conv_w.shape, const), _resident(conv_p.shape, const)]
    if sample:
        args.append(state)
        specs.append(_resident(state.shape, lambda i: (0, 0, 0)))

    def out(width, dtype):
        return jax.ShapeDtypeStruct((m, width), dtype), pl.BlockSpec((tm, width), row)

    outs = [out(c_conv, BF16), out(n_heads * LANES, BF16), out(kvw, F32), out(kvw, F32),
            out(kvw, BF16), out(N_KV_HEADS * LANES, BF16), out(H_IDX * D_IDX, BF16), out(LANES, BF16),
            out(LANES, BF16), out(D_IDX, F32), out(H_IDX, F32)]
    if sample:
        outs.append(out(c_conv, F32))
        scratch = []
    else:
        outs.append((jax.ShapeDtypeStruct((n_seq, CONV_HALO, c_conv), F32),
                     pl.BlockSpec((1, CONV_HALO, c_conv), lambda i: (i // tiles_per_seq, 0, 0))))
        scratch = [pltpu.VMEM((tm + CONV_HALO, c_conv), F32), pltpu.VMEM((tm, c_conv), F32),
                   pltpu.VMEM((tm + CONV_HALO, LANES), F32)]

    return pl.pallas_call(
        functools.partial(_proj_kernel, tm=tm, tiles_per_seq=tiles_per_seq, sample=sample, d_mix=d_mix),
        grid=(nt,),
        in_specs=specs,
        out_specs=[o[1] for o in outs],
        out_shape=[o[0] for o in outs],
        scratch_shapes=scratch,
        compiler_params=_cparams(("arbitrary",)),
        name=name,
    )(*args)


def _key_to_f32(key):
    bits = key ^ ((key >> 31) & jnp.int32(0x7FFFFFFF))
    return lax.bitcast_convert_type(bits, F32)


def _chunk_loop(n, body, carry, unroll=2):
    assert unroll & (unroll - 1) == 0

    def run(first, count, carry):
        for u in range(count):
            carry = body(first + u, carry)
        return carry

    carry = lax.fori_loop(0, n // unroll, lambda i, c: run(i * unroll, unroll, c), carry)
    done = n // unroll * unroll
    part = unroll // 2
    while part >= 1:
        carry = lax.cond((n - done) // part % 2 == 1, lambda c, d=done, p=part: run(d, p, c), lambda c: c, carry)
        done = done + jnp.where((n - done) // part % 2 == 1, part, 0)
        part //= 2
    return carry


def _count(sc_ref, nchunks, ck, pred, whole=False):
    rows = sc_ref.shape[0]

    def body(c, acc):
        start = pl.multiple_of(c * ck, ck)
        hit = jnp.where(pred(sc_ref[:, pl.ds(start, ck)], start), 1.0, 0.0)
        part = hit[:, 0:LANES]
        for i in range(1, ck // LANES):
            part = part + hit[:, i * LANES:(i + 1) * LANES]
        return acc + part

    acc = lax.fori_loop(0, nchunks, body, jnp.zeros((rows, LANES), F32))
    per_row = jnp.sum(acc, axis=1, keepdims=True)
    return jnp.sum(per_row, axis=0, keepdims=True) if whole else per_row


def _position(shape, start, whole):
    col = start + lax.broadcasted_iota(I32, shape, 1)
    return lax.broadcasted_iota(I32, shape, 0) * shape[1] + col if whole else col


def _exact_threshold(sc_ref, nchunks, ck, n_sel, n_adm, total_cols, whole=False):
    rows = 1 if whole else sc_ref.shape[0]
    want = jnp.float32(n_sel)

    def bisect(i, carry):
        t, ct = carry
        cand = t + lax.shift_left(jnp.int32(1), 31 - i)
        thr = _key_to_f32(cand)
        cnt = _count(sc_ref, nchunks, ck, lambda blk, _: blk >= thr, whole)
        take = cnt >= want
        return jnp.where(take, cand, t), jnp.where(take, cnt, ct)

    t, ct = lax.fori_loop(0, 32, bisect, (jnp.full((rows, 1), INT_MIN, I32), jnp.zeros((rows, 1), F32)))
    full = n_adm <= n_sel
    thr = jnp.where(full, F32_LOWEST, _key_to_f32(t))
    tied = jnp.logical_and(jnp.logical_not(full), ct > want)

    @pl.when(jnp.sum(jnp.where(tied, 1.0, 0.0)) > 0.0)
    def _():
        n_gt = _count(sc_ref, nchunks, ck, lambda blk, _: blk > thr, whole)
        room = want - n_gt
        nbits = max(int(total_cols - 1).bit_length(), 1)

        def search(i, q):
            cand = q + lax.shift_left(jnp.int32(1), nbits - 1 - i)
            below = _count(sc_ref, nchunks, ck,
                           lambda blk, s: jnp.logical_and(blk == thr, _position(blk.shape, s, whole) < cand), whole)
            return jnp.where(below < room, cand, q)

        last = lax.fori_loop(0, nbits, search, jnp.zeros((rows, 1), I32))

        def drop(c, _):
            start = pl.multiple_of(c * ck, ck)
            blk = sc_ref[:, pl.ds(start, ck)]
            lose = jnp.logical_and(tied, jnp.logical_and(blk == thr, _position(blk.shape, start, whole) > last))
            sc_ref[:, pl.ds(start, ck)] = jnp.where(lose, -jnp.inf, blk)
            return 0

        lax.fori_loop(0, nchunks, drop, 0)

    return thr


def _select_threshold(sc_ref, thr_ref, nchunks, ck, n_sel, n_adm, total_cols, row_lo, row_hi, whole=False):
    rows = 1 if whole else sc_ref.shape[0]
    want = jnp.float32(n_sel)
    full = n_adm <= n_sel

    def unresolved(done):
        return jnp.sum(done) < rows

    def cond(state):
        it, go = state[0], state[1]
        return jnp.logical_and(it < SEARCH_CAP, go)

    def body(state):
        it, _, lo, hi, thr, done = state
        mid = lo + (hi - lo) * 0.5
        cnt = _count(sc_ref, nchunks, ck, lambda blk, _: blk >= mid, whole)
        go = unresolved(done)
        found = cnt == want
        thr = jnp.where(jnp.logical_and(found, done == 0.0), mid, thr)
        done = jnp.where(found, 1.0, done)
        return it + 1, go, jnp.where(cnt > want, mid, lo), jnp.where(cnt < want, mid, hi), thr, done

    done0 = jnp.where(full, 1.0, 0.0)
    init = (jnp.int32(0), unresolved(done0), row_lo, row_hi, jnp.where(full, F32_LOWEST, row_lo), done0)
    _, _, _, _, thr, done = lax.while_loop(cond, body, init)
    thr_ref[...] = thr

    @pl.when(unresolved(done))
    def _():
        exact = _exact_threshold(sc_ref, nchunks, ck, n_sel, n_adm, total_cols, whole)
        thr_ref[...] = jnp.where(done > 0.0, thr, exact)

    return thr_ref[...]


def _attn_kernel(qi_ref, w_ref, kia_ref, kib_ref, q_ref, k_ref, v_ref, o_ref, sc_ref, thr_ref, s_ref, wb_ref,
                 *, tq, th, n_sel, seq_len):
    t0 = pl.program_id(1) * tq
    nk = t0 // ATT_CK + 1
    row_pos = t0 + lax.broadcasted_iota(I32, (tq, 1), 0)
    n_heads = q_ref.shape[2] // LANES
    group = n_heads // N_KV_HEADS

    def fold(op, acc, x):
        for i in range(x.shape[1] // LANES):
            acc = op(acc, x[:, i * LANES:(i + 1) * LANES])
        return acc

    for h in range(H_IDX):
        wb_ref[h] = jnp.broadcast_to(w_ref[0, :, h:h + 1], (tq, LANES))

    def head_weight(h):
        return jnp.concatenate([wb_ref[h]] * (ATT_CK_SCORE // LANES), axis=1)

    def score(c, carry):
        hi_run, lo_run = carry
        start = pl.multiple_of(c * ATT_CK_SCORE, ATT_CK_SCORE)
        ka = kia_ref[0, pl.ds(start, ATT_CK_SCORE), :]
        kb = kib_ref[0, pl.ds(start, ATT_CK_SCORE), :]
        acc = jnp.zeros((tq, ATT_CK_SCORE), F32)
        for j in range(H_IDX // 2):
            pair = qi_ref[0, :, j * LANES:(j + 1) * LANES]
            acc = acc + jnp.maximum(_dot_nt(pair, ka), 0.0) * head_weight(2 * j)
            acc = acc + jnp.maximum(_dot_nt(pair, kb), 0.0) * head_weight(2 * j + 1)
        admissible = start + lax.broadcasted_iota(I32, (tq, ATT_CK_SCORE), 1) <= row_pos
        masked = jnp.where(admissible, acc, -jnp.inf)
        sc_ref[:, pl.ds(start, ATT_CK_SCORE)] = masked
        return fold(jnp.maximum, hi_run, masked), fold(jnp.minimum, lo_run, jnp.where(admissible, acc, jnp.inf))

    hi_run, lo_run = _chunk_loop(nk * (ATT_CK // ATT_CK_SCORE), score,
                                 (jnp.full((tq, LANES), -jnp.inf, F32), jnp.full((tq, LANES), jnp.inf, F32)),
                                 unroll=SCORE_UNROLL)

    thr = _select_threshold(sc_ref, thr_ref, nk, ATT_CK, n_sel, row_pos + 1, seq_len,
                            jnp.min(lo_run, axis=1, keepdims=True), jnp.max(hi_run, axis=1, keepdims=True))

    def to_mask(c, _):
        start = pl.multiple_of(c * ATT_CK, ATT_CK)
        sc_ref[:, pl.ds(start, ATT_CK)] = jnp.where(sc_ref[:, pl.ds(start, ATT_CK)] >= thr, 0.0, NEG_BIG)
        return 0

    lax.fori_loop(0, nk, to_mask, 0)

    rows = group * th

    def attend(u, _):
        r0 = pl.multiple_of(u // N_KV_HEADS * th, th)
        n = u % N_KV_HEADS
        kv_lanes = pl.ds(pl.multiple_of(n // 2 * LANES, LANES), LANES)
        qg = jnp.concatenate(
            [q_ref[0, pl.ds(r0, th), pl.ds(pl.multiple_of((group * n + g) * LANES, LANES), LANES)]
             for g in range(group)], axis=0)

        def logits(c, m_run):
            start = pl.multiple_of(c * ATT_CK, ATT_CK)
            s = (_dot_nt(qg, k_ref[0, pl.ds(start, ATT_CK), kv_lanes])
                 + jnp.concatenate([sc_ref[pl.ds(r0, th), pl.ds(start, ATT_CK)]] * group, axis=0))
            s_ref[:, pl.ds(start, ATT_CK)] = s
            return fold(jnp.maximum, m_run, s)

        m_run = _chunk_loop(nk, logits, jnp.full((rows, LANES), -jnp.inf, F32), unroll=ATT_UNROLL)
        m = jnp.max(m_run, axis=1, keepdims=True)

        v_lanes = pl.ds(pl.multiple_of(n * LANES, LANES), LANES)

        def weigh(c, acc):
            start = pl.multiple_of(c * ATT_CK, ATT_CK)
            p = jnp.exp(s_ref[:, pl.ds(start, ATT_CK)] - m)
            return acc + _dot(p.astype(BF16), v_ref[0, pl.ds(start, ATT_CK), v_lanes])

        acc = _chunk_loop(nk, weigh, jnp.zeros((rows, LANES), F32), unroll=ATT_UNROLL)
        out = acc[:, :HEAD_DIM] / acc[:, HEAD_DIM:]
        width = group * HEAD_DIM
        o_ref[0, pl.ds(r0, th), pl.ds(pl.multiple_of(n * width, width), width)] = jnp.concatenate(
            [out[g * th:(g + 1) * th] for g in range(group)], axis=1).astype(o_ref.dtype)
        return 0

    lax.fori_loop(0, tq // th * N_KV_HEADS, attend, 0)


def _prompt_attention(qi, wi, kia, kib, qpad, kb, vb, *, n_sel):
    b, s, _ = qi.shape
    tq = min(ATT_TQ, s)
    th = min(ATT_TH, tq)
    assert s % ATT_CK == 0 and ATT_CK % tq == 0 and ATT_CK % ATT_CK_SCORE == 0 and tq % th == 0
    n_heads = qpad.shape[2] // LANES
    tile = lambda bi, i: (bi, i, 0)
    seq = lambda bi, i: (bi, 0, 0)
    return pl.pallas_call(
        functools.partial(_attn_kernel, tq=tq, th=th, n_sel=n_sel, seq_len=s),
        grid=(b, s // tq),
        in_specs=[pl.BlockSpec((1, tq, qi.shape[2]), tile), pl.BlockSpec((1, tq, wi.shape[2]), tile),
                  _resident((1, s, LANES), seq), _resident((1, s, LANES), seq),
                  pl.BlockSpec((1, tq, qpad.shape[2]), tile),
                  _resident((1, s, kb.shape[2]), seq), _resident((1, s, vb.shape[2]), seq)],
        out_specs=pl.BlockSpec((1, tq, n_heads * HEAD_DIM), tile),
        out_shape=jax.ShapeDtypeStruct((b, s, n_heads * HEAD_DIM), BF16),
        scratch_shapes=[pltpu.VMEM((tq, s), F32), pltpu.VMEM((tq, 1), F32),
                        pltpu.VMEM((n_heads // N_KV_HEADS * th, s), F32), pltpu.VMEM((H_IDX, tq, LANES), F32)],
        compiler_params=_cparams(("arbitrary", "arbitrary")),
        name="prompt_attention",
    )(qi, wi, kia, kib, qpad, kb, vb)


def _native_pages(cache):
    rank = cache.ndim
    t = jnp.transpose(cache, (0, 1) + tuple(range(3, rank)) + (2,))
    return t.reshape(-1, cache.shape[2])


def _slot0_page(x, page):
    return jnp.pad(x[:, :, None], ((0, 0), (0, 0), (0, page - 1)))


def _sample_select_kernel(pt_ref, qi_ref, w_ref, kin_ref, cache_ref, mask_ref, buf_ref, row_ref, sc_ref, thr_ref, sem,
                          *, layer, n_pool, n_pages, page, n_sel, rows):
    b = pl.program_id(0)
    past = n_pages * page
    total = rows * LANES

    def page_copy(j):
        src = pl.ds((layer * n_pool + pt_ref[b, j]) * D_IDX, D_IDX)
        return pltpu.make_async_copy(cache_ref.at[src], buf_ref.at[:, pl.ds(j * page, page)], sem)

    def start(j, _):
        page_copy(j).start()
        return 0

    def wait(j, _):
        page_copy(j).wait()
        return 0

    lax.fori_loop(0, n_pages, start, 0)

    @pl.when(b == 0)
    def _():
        buf_ref[:, past + page:] = jnp.zeros((D_IDX, total - past - page), F32)

    buf_ref[:, past:past + page] = kin_ref[0]
    lax.fori_loop(0, n_pages, wait, 0)

    x = _dot(qi_ref[0], buf_ref[...].astype(BF16))
    score = jnp.sum(jnp.maximum(x, 0.0) * w_ref[0], axis=0, keepdims=True)
    admissible = lax.broadcasted_iota(I32, (1, total), 1) <= past
    masked = jnp.where(admissible, score, -jnp.inf)
    row_ref[...] = masked
    for r in range(rows):
        sc_ref[r:r + 1, :] = row_ref[:, r * LANES:(r + 1) * LANES]
    thr = _select_threshold(sc_ref, thr_ref, 1, LANES, n_sel, jnp.full((1, 1), past + 1, I32), total,
                            jnp.min(jnp.where(admissible, score, jnp.inf), axis=1, keepdims=True),
                            jnp.max(masked, axis=1, keepdims=True), whole=True)
    mask_ref[0] = jnp.where(sc_ref[...] >= thr, 0.0, NEG_BIG)


def _sample_select(page_table, qi, wi, ki_new, cache_idx, *, layer, n_sel):
    db, n_pages = page_table.shape
    depth, n_pool, page, _ = cache_idx.shape
    assert page == LANES
    past = n_pages * page
    total = _round_up(past + page, SUBLANES * LANES)
    rows = total // LANES
    grid_spec = pltpu.PrefetchScalarGridSpec(
        num_scalar_prefetch=1,
        grid=(db,),
        in_specs=[pl.BlockSpec((1, H_IDX, D_IDX), lambda b, pt: (b, 0, 0)),
                  pl.BlockSpec((1, H_IDX, 1), lambda b, pt: (b, 0, 0)),
                  pl.BlockSpec((1, D_IDX, page), lambda b, pt: (b, 0, 0)),
                  pl.BlockSpec(memory_space=pl.ANY)],
        out_specs=pl.BlockSpec((1, rows, LANES), lambda b, pt: (b, 0, 0)),
        scratch_shapes=[pltpu.VMEM((D_IDX, total), F32), pltpu.VMEM((1, total), F32),
                        pltpu.VMEM((rows, LANES), F32), pltpu.VMEM((1, 1), F32), pltpu.SemaphoreType.DMA(())],
    )
    return pl.pallas_call(
        functools.partial(_sample_select_kernel, layer=layer, n_pool=n_pool, n_pages=n_pages, page=page, n_sel=n_sel,
                          rows=rows),
        grid_spec=grid_spec,
        out_shape=jax.ShapeDtypeStruct((db, rows, LANES), F32),
        compiler_params=_cparams(("arbitrary",)),
        name="sample_select",
    )(page_table, qi, wi, _slot0_page(ki_new, page), _native_pages(cache_idx))


def _sample_attend_kernel(pt_ref, mask_ref, qt_ref, knew_ref, vnew_ref, ck_ref, cv_ref, o_ref,
                          kbuf, vbuf, s_ref, qb_ref, sem_k, sem_v, *, layer, n_pool, n_pages, page, rows):
    b = pl.program_id(0)
    n_heads = qt_ref.shape[2]
    group = n_heads // N_KV_HEADS
    kvw = N_KV_HEADS * HEAD_DIM
    live = n_pages + 1

    def fetch(cache_ref, new_ref, buf, sem, sample, wait):
        def one(src, j):
            cp = pltpu.make_async_copy(src, buf.at[pl.ds(j * kvw, kvw)], sem)
            cp.wait() if wait else cp.start()

        def cached(j, _):
            one(cache_ref.at[pl.ds((layer * n_pool + pt_ref[sample, j]) * kvw, kvw)], j)
            return 0

        lax.fori_loop(0, n_pages, cached, 0)
        one(new_ref.at[sample], n_pages)

    k_pages = functools.partial(fetch, ck_ref, knew_ref, kbuf, sem_k)
    v_pages = functools.partial(fetch, cv_ref, vnew_ref, vbuf, sem_v)
    more = b + 1 < pl.num_programs(0)

    @pl.when(b == 0)
    def _():
        k_pages(0, wait=False)
        v_pages(0, wait=False)

    for h in range(n_heads):
        qb_ref[h] = jnp.broadcast_to(qt_ref[0, :, h:h + 1], (HEAD_DIM, LANES))

    s_ref[:, live * page:] = jnp.full((n_heads, (rows - live) * page), NEG_BIG, F32)
    k_pages(b, wait=True)

    def logits(j, _):
        row0 = pl.multiple_of(j * kvw, kvw)
        col0 = pl.multiple_of(j * page, page)
        drop = mask_ref[0, pl.ds(j, 1), :]
        for n in range(N_KV_HEADS):
            k_n = kbuf[pl.ds(row0 + n * HEAD_DIM, HEAD_DIM), :]
            for g in range(group):
                h = n * group + g
                s_ref[h:h + 1, pl.ds(col0, page)] = jnp.sum(k_n * qb_ref[h], axis=0, keepdims=True) + drop
        return 0

    lax.fori_loop(0, live, logits, 0, unroll=PAGE_UNROLL)

    @pl.when(more)
    def _():
        k_pages(b + 1, wait=False)

    s = s_ref[...]
    p = jnp.exp(s - jnp.max(s, axis=1, keepdims=True))
    s_ref[...] = p / jnp.sum(p, axis=1, keepdims=True)
    v_pages(b, wait=True)

    for h in range(n_heads):
        def weigh(j, acc):
            rows_h = pl.ds(pl.multiple_of(j * kvw, kvw) + h // group * HEAD_DIM, HEAD_DIM)
            return acc + vbuf[rows_h, :] * s_ref[h:h + 1, pl.ds(pl.multiple_of(j * page, page), page)]

        acc = lax.fori_loop(0, live, weigh, jnp.zeros((HEAD_DIM, LANES), F32), unroll=PAGE_UNROLL)
        o_ref[0, :, h:h + 1] = jnp.sum(acc, axis=1, keepdims=True)

    @pl.when(more)
    def _():
        v_pages(b + 1, wait=False)


def _sample_attend(page_table, mask, q, k_new, v_new, cache_k, cache_v, *, layer):
    db, n_pages = page_table.shape
    depth, n_pool, page = cache_k.shape[:3]
    n_heads = q.shape[1]
    rows = mask.shape[1]
    kvw = N_KV_HEADS * HEAD_DIM
    assert page == LANES and rows > n_pages
    grid_spec = pltpu.PrefetchScalarGridSpec(
        num_scalar_prefetch=1,
        grid=(db,),
        in_specs=[pl.BlockSpec((1, rows, LANES), lambda b, pt: (b, 0, 0)),
                  pl.BlockSpec((1, HEAD_DIM, n_heads), lambda b, pt: (b, 0, 0)),
                  pl.BlockSpec(memory_space=pl.ANY), pl.BlockSpec(memory_space=pl.ANY),
                  pl.BlockSpec(memory_space=pl.ANY), pl.BlockSpec(memory_space=pl.ANY)],
        out_specs=pl.BlockSpec((1, HEAD_DIM, n_heads), lambda b, pt: (b, 0, 0)),
        scratch_shapes=[pltpu.VMEM(((n_pages + 1) * kvw, page), F32), pltpu.VMEM(((n_pages + 1) * kvw, page), F32),
                        pltpu.VMEM((n_heads, rows * LANES), F32), pltpu.VMEM((n_heads, HEAD_DIM, LANES), F32),
                        pltpu.SemaphoreType.DMA(()), pltpu.SemaphoreType.DMA(())],
    )
    o_t = pl.pallas_call(
        functools.partial(_sample_attend_kernel, layer=layer, n_pool=n_pool, n_pages=n_pages, page=page, rows=rows),
        grid_spec=grid_spec,
        out_shape=jax.ShapeDtypeStruct((db, HEAD_DIM, n_heads), F32),
        compiler_params=_cparams(("arbitrary",)),
        name="sample_attend",
    )(page_table, mask, jnp.swapaxes(q, 1, 2).astype(F32), _slot0_page(k_new, page), _slot0_page(v_new, page),
      _native_pages(cache_k), _native_pages(cache_v))
    return jnp.swapaxes(o_t, 1, 2).reshape(db, n_heads * HEAD_DIM)


def _pack_ffn(w_in, w_out):
    d, f2 = w_in.shape
    f = f2 // 2
    fp = _round_up(f, FFN_TF)
    pad = lambda w, axis: jnp.pad(w, [(0, fp - f) if a == axis else (0, 0) for a in range(2)])
    w_in = w_in.astype(BF16)
    return pad(w_in[:, :f], 1), pad(w_in[:, f:], 1), pad(w_out.astype(BF16), 0)


def _pack_proj(w_in, d_mix):
    d = w_in.shape[0]
    c_conv = d_mix // C_CONV_FRACTION
    d_attn = d_mix - c_conv
    n_heads = d_attn // HEAD_DIM
    kvw = N_KV_HEADS * HEAD_DIM
    sizes = [2 * c_conv, d_attn, kvw, kvw, H_IDX * D_IDX, D_IDX, H_IDX]
    parts, o = [], 0
    for sz in sizes:
        parts.append(w_in[:, o:o + sz])
        o += sz
    p_conv, q, k, v, qi, ki, wi = parts
    qh = q.reshape(d, n_heads, HEAD_DIM)
    zero = jnp.zeros_like(qh)
    odd = ((jnp.arange(n_heads) // (n_heads // N_KV_HEADS)) % 2 == 1)[None, :, None]
    qpad = jnp.concatenate([jnp.where(odd, zero, qh), jnp.where(odd, qh, zero)], axis=-1).reshape(d, n_heads * LANES)
    z = jnp.zeros((d, LANES - D_IDX), w_in.dtype)
    cols = [p_conv[:, :c_conv], p_conv[:, c_conv:], qpad, k, v, qi,
            jnp.concatenate([ki, z], axis=1), jnp.concatenate([z, ki], axis=1),
            jnp.concatenate([wi, jnp.zeros((d, LANES - H_IDX), w_in.dtype)], axis=1)]
    w_all = jnp.concatenate(cols, axis=1).astype(BF16)
    assert w_all.shape[1] == _proj_layout(d_mix)[1]
    return w_all


def _own_half(o_pad, n_heads):
    odd = ((jnp.arange(n_heads) // (n_heads // N_KV_HEADS)) % 2 == 1)[:, None]
    return jnp.where(odd, o_pad[..., HEAD_DIM:], o_pad[..., :HEAD_DIM])


def kernel(x_prompt, x_sample, cache_k, cache_v, cache_idx_k, state_conv, page_table, norm_ffn1, ffn1_w_in, ffn1_w_out, norm_mix, w_in, conv_w, conv_b, conv_ln_g, conv_ln_b, w_out, norm_ffn2, ffn2_w_in, ffn2_w_out, norm_final):
    depth = w_in.shape[0]
    bsz, seq, d = x_prompt.shape
    db, dseq, _ = x_sample.shape
    assert dseq == 1
    d_mix = d
    c_conv = d_mix // C_CONV_FRACTION
    n_heads = (d_mix - c_conv) // HEAD_DIM
    page = cache_k.shape[2]
    n_pages = page_table.shape[1]
    past = n_pages * page
    n_sel_p = min(TOPK_MAX, seq // 4)
    n_sel_s = min(TOPK_MAX, (past + dseq) // 4)

    xp = x_prompt.reshape(bsz * seq, d)
    xs = x_sample.reshape(db, d)
    g_final = norm_final.reshape(1, d)
    outs_p, outs_s = [], []

    for l in range(depth):
        f1 = _pack_ffn(ffn1_w_in[l], ffn1_w_out[l])
        f2 = _pack_ffn(ffn2_w_in[l], ffn2_w_out[l])
        w_all = _pack_proj(w_in[l], d_mix)
        conv_p = jnp.stack([conv_b[l], conv_ln_g[l], conv_ln_b[l]])
        wo = w_out[l].astype(BF16)
        woc, woa = wo[:c_conv], wo[c_conv:]
        g1, gm, g2 = norm_ffn1[l].reshape(1, d), norm_mix[l].reshape(1, d), norm_ffn2[l].reshape(1, d)
        last = g_final if l == depth - 1 else None

        xp = _ffn(xp, g1, *f1, name="ffn1_prompt")
        (conv_o, qpad, k_new, v_new, kb, vb, qi, kia, kib, ki_new, wi, u_tail) = _proj(
            xp, gm, w_all, conv_w[l], conv_p, seq_len=seq, name="proj_prompt")
        r3 = lambda a: a.reshape(bsz, seq, a.shape[-1])
        attn = _prompt_attention(r3(qi), r3(wi), r3(kia), r3(kib), r3(qpad), r3(kb), r3(vb), n_sel=n_sel_p)
        xp = _ffn(xp, g2, *f2, mix=(conv_o, attn.reshape(bsz * seq, -1), woc, woa), g_final=last,
                  name="ffn2_prompt")
        outs_p.append((k_new.reshape(bsz, seq, N_KV_HEADS, HEAD_DIM), v_new.reshape(bsz, seq, N_KV_HEADS, HEAD_DIM),
                       ki_new.reshape(bsz, seq, D_IDX), u_tail[:, CONV_HALO - (CONV_WIDTH - 1):, :]))

        state = state_conv[l].astype(F32)
        xs = _ffn(xs, g1, *f1, name="ffn1_sample")
        (conv_o, qpad, k_new, v_new, _, _, qi, _, _, ki_new, wi, u_new) = _proj(
            xs, gm, w_all, conv_w[l], conv_p, seq_len=1, state=jnp.swapaxes(state, 0, 1), name="proj_sample")
        mask = _sample_select(page_table, qi.reshape(db, H_IDX, D_IDX), wi.reshape(db, H_IDX, 1), ki_new,
                              cache_idx_k, layer=l, n_sel=n_sel_s)
        attn = _sample_attend(page_table, mask, _own_half(qpad.reshape(db, n_heads, LANES), n_heads), k_new, v_new,
                              cache_k, cache_v, layer=l).astype(BF16)
        xs = _ffn(xs, g2, *f2, mix=(conv_o, attn, woc, woa), g_final=last, name="ffn2_sample")
        outs_s.append((k_new.reshape(db, 1, N_KV_HEADS, HEAD_DIM), v_new.reshape(db, 1, N_KV_HEADS, HEAD_DIM),
                       ki_new.reshape(db, 1, D_IDX),
                       jnp.concatenate([state[:, 1:, :], u_new[:, None, :]], axis=1)))

    stack = lambda outs, i: jnp.stack([o[i] for o in outs])
    return (xp.reshape(bsz, seq, d), xs.reshape(db, 1, d),
            stack(outs_p, 0), stack(outs_p, 1), stack(outs_p, 2), stack(outs_p, 3),
            stack(outs_s, 0), stack(outs_s, 1), stack(outs_s, 2), stack(outs_s, 3))
```

```python
import functools

import jax
import jax.numpy as jnp
from jax import lax
from jax.experimental import pallas as pl
from jax.experimental.pallas import tpu as pltpu

F32 = jnp.float32
BF16 = jnp.bfloat16
I32 = jnp.int32

C_CONV_FRACTION = 2
CONV_WIDTH = 31
HEAD_DIM = 64
N_KV_HEADS = 4
H_IDX = 16
D_IDX = 64
TOPK_MAX = 256
EPS = 1e-6
ATTN_SCALE = HEAD_DIM ** -0.5
IDX_SCALE = (D_IDX ** -0.5) * (H_IDX ** -0.5)

LANES = 128
SUBLANES = 8
VMEM_LIMIT_BYTES = 56 * 1024 * 1024

FFN_TM = 512
FFN_TF = 512
PROJ_TM = 256
CONV_HALO = 32
ATT_TQ = 128
ATT_TH = 128
ATT_CK_SCORE = 256
ATT_CK = 512
ATT_UNROLL = 8
SCORE_UNROLL = 4
COUNT_UNROLL = 1
SEARCH_CAP = 40
PAGE_UNROLL = 3

INT_MIN = -2 ** 31
NEG_BIG = -1e30
F32_LOWEST = float(jnp.finfo(jnp.float32).min)


def _round_up(x, m):
    return (x + m - 1) // m * m


def _cparams(sem):
    return pltpu.CompilerParams(dimension_semantics=sem, vmem_limit_bytes=VMEM_LIMIT_BYTES)


def _resident(shape, index_map):
    return pl.BlockSpec(shape, index_map, pipeline_mode=pl.Buffered(1))


def _rms(x, g):
    ms = jnp.mean(x * x, axis=-1, keepdims=True)
    return x * lax.rsqrt(ms + EPS) * g


def _dot(a, b):
    return jnp.dot(a, b, preferred_element_type=F32)


def _dot_nt(a, b):
    return lax.dot_general(a, b, (((1,), (1,)), ((), ())), preferred_element_type=F32)


def _ffn_kernel(*refs, has_mix, has_final):
    it = iter(refs)
    x_ref = next(it)
    if has_mix:
        mc_ref, ma_ref, woc_ref, woa_ref = next(it), next(it), next(it), next(it)
    g_ref, wa_ref, wb_ref, wo_ref = next(it), next(it), next(it), next(it)
    if has_final:
        gf_ref = next(it)
    o_ref = next(it)
    xn_ref = next(it)

    f = pl.program_id(1)

    @pl.when(f == 0)
    def _():
        x = x_ref[...]
        if has_mix:
            x = x + _dot(mc_ref[...], woc_ref[...]) + _dot(ma_ref[...], woa_ref[...])
        o_ref[...] = x
        xn_ref[...] = _rms(x, g_ref[...]).astype(BF16)

    xn = xn_ref[...]
    a = _dot(xn, wa_ref[...])
    b = _dot(xn, wb_ref[...])
    act = (a * jax.nn.sigmoid(a) * b).astype(BF16)
    o_ref[...] += 0.5 * _dot(act, wo_ref[...])

    if has_final:
        @pl.when(f == pl.num_programs(1) - 1)
        def _():
            o_ref[...] = _rms(o_ref[...], gf_ref[...])


def _ffn(x, g, wa, wb, wo, *, mix=None, g_final=None, name):
    m, d = x.shape
    fp = wa.shape[1]
    tm = min(FFN_TM, m)
    assert m % tm == 0 and fp % FFN_TF == 0
    nf = fp // FFN_TF
    has_mix = mix is not None
    has_final = g_final is not None

    row = lambda i, f: (i, 0)
    const = lambda i, f: (0, 0)
    args = [x]
    specs = [pl.BlockSpec((tm, d), row)]
    if has_mix:
        mc, ma, woc, woa = mix
        args += [mc, ma, woc, woa]
        specs += [pl.BlockSpec((tm, mc.shape[1]), row), pl.BlockSpec((tm, ma.shape[1]), row),
                  _resident(woc.shape, const), _resident(woa.shape, const)]
    args += [g, wa, wb, wo]
    specs += [_resident((1, d), const),
              pl.BlockSpec((d, FFN_TF), lambda i, f: (0, f)),
              pl.BlockSpec((d, FFN_TF), lambda i, f: (0, f)),
              pl.BlockSpec((FFN_TF, d), lambda i, f: (f, 0))]
    if has_final:
        args.append(g_final)
        specs.append(_resident((1, d), const))

    return pl.pallas_call(
        functools.partial(_ffn_kernel, has_mix=has_mix, has_final=has_final),
        grid=(m // tm, nf),
        in_specs=specs,
        out_specs=pl.BlockSpec((tm, d), row),
        out_shape=jax.ShapeDtypeStruct((m, d), F32),
        scratch_shapes=[pltpu.VMEM((tm, d), BF16)],
        compiler_params=_cparams(("arbitrary", "arbitrary")),
        name=name,
    )(*args)


def _proj_layout(d_mix):
    c_conv = d_mix // C_CONV_FRACTION
    d_attn = d_mix - c_conv
    n_heads = d_attn // HEAD_DIM
    widths = dict(ca=c_conv, cg=c_conv, qpad=n_heads * LANES, k=N_KV_HEADS * HEAD_DIM,
                  v=N_KV_HEADS * HEAD_DIM, qi=H_IDX * D_IDX, kia=LANES, kib=LANES, wi=LANES)
    off, o = {}, 0
    for name, w in widths.items():
        off[name] = (o, o + w)
        o += w
    return off, o


def _proj_kernel(*refs, tm, tiles_per_seq, sample, d_mix):
    it = iter(refs)
    x_ref, g_ref, w_ref, cw_ref, cp_ref = next(it), next(it), next(it), next(it), next(it)
    if sample:
        st_ref = next(it)
    (conv_ref, qpad_ref, k_ref, v_ref, kb_ref, vb_ref, qi_ref, kia_ref, kib_ref, ki_ref, wi_ref,
     u_ref) = (next(it) for _ in range(12))
    if not sample:
        win_ref, y_ref, z_ref = next(it), next(it), next(it)

    off, _ = _proj_layout(d_mix)
    c_conv = d_mix // C_CONV_FRACTION

    def cols(name):
        lo, hi = off[name]
        return w_ref[:, lo:hi]

    if not sample:
        @pl.when(pl.program_id(0) % tiles_per_seq == 0)
        def _():
            win_ref[0:CONV_HALO, :] = jnp.zeros((CONV_HALO, c_conv), F32)

    xn = _rms(x_ref[...], g_ref[...]).astype(BF16)
    u = _dot(xn, cols("ca")) * jax.nn.sigmoid(_dot(xn, cols("cg")))

    bias = cp_ref[0:1, :]
    ln_g = cp_ref[1:2, :]
    ln_b = cp_ref[2:3, :]

    if sample:
        u_ref[...] = u
        y = bias + cw_ref[CONV_WIDTH - 1:CONV_WIDTH, :] * u
        for j in range(CONV_WIDTH - 1):
            y = y + cw_ref[j:j + 1, :] * st_ref[j]
    else:
        win_ref[CONV_HALO:CONV_HALO + tm, :] = u
        first = CONV_HALO - (CONV_WIDTH - 1)
        for c in range(c_conv // LANES):
            cs = slice(c * LANES, (c + 1) * LANES)
            acc = jnp.zeros((tm, LANES), F32) + bias[:, cs]
            for r in range(SUBLANES):
                taps = [j for j in range(CONV_WIDTH) if (first + j) % SUBLANES == r]
                if taps:
                    base = first + taps[0]
                    span = first + taps[-1] + tm - base
                    if r:
                        z_ref[0:span, :] = win_ref[base:base + span, cs]
                    for j in taps:
                        lo = first + j - base
                        z = z_ref[lo:lo + tm, :] if r else win_ref[first + j:first + j + tm, cs]
                        acc = acc + cw_ref[j:j + 1, cs] * z
            y_ref[:, cs] = acc
        y = y_ref[...]
        tail = win_ref[tm:tm + CONV_HALO, :]
        u_ref[0] = tail
        win_ref[0:CONV_HALO, :] = tail

    mu = jnp.mean(y, axis=-1, keepdims=True)
    var = jnp.mean(jnp.square(y - mu), axis=-1, keepdims=True)
    yn = (y - mu) * lax.rsqrt(var + EPS) * ln_g + ln_b
    conv_ref[...] = (yn * jax.nn.sigmoid(yn)).astype(BF16)

    qpad_ref[...] = (_dot(xn, cols("qpad")) * ATTN_SCALE).astype(BF16)
    kk = _dot(xn, cols("k"))
    vv = _dot(xn, cols("v"))
    k_ref[...] = kk
    v_ref[...] = vv
    kb_ref[...] = kk.astype(BF16)
    ones = jnp.ones((vv.shape[0], LANES - HEAD_DIM), F32)
    vb_ref[...] = jnp.concatenate(
        [t for n in range(N_KV_HEADS) for t in (vv[:, n * HEAD_DIM:(n + 1) * HEAD_DIM], ones)], axis=1).astype(BF16)
    qi_ref[...] = _dot(xn, cols("qi")).astype(BF16)
    kia = _dot(xn, cols("kia"))
    kia_ref[...] = kia.astype(BF16)
    kib_ref[...] = _dot(xn, cols("kib")).astype(BF16)
    ki_ref[...] = kia[:, :D_IDX]
    wi_ref[...] = _dot(xn, cols("wi"))[:, :H_IDX] * IDX_SCALE


def _proj(x, g, w_all, conv_w, conv_p, *, seq_len, state=None, name):
    m, d = x.shape
    d_mix = d
    c_conv = d_mix // C_CONV_FRACTION
    n_heads = (d_mix - c_conv) // HEAD_DIM
    kvw = N_KV_HEADS * HEAD_DIM
    sample = state is not None
    tm = m if sample else min(PROJ_TM, seq_len)
    assert m % tm == 0 and seq_len % tm == 0 or sample
    assert tm >= CONV_HALO or sample
    nt = m // tm
    tiles_per_seq = max(seq_len // tm, 1)
    n_seq = m // seq_len

    row = lambda i: (i, 0)
    const = lambda i: (0, 0)
    args = [x, g, w_all, conv_w, conv_p]
    specs = [pl.BlockSpec((tm, d), row), _resident((1, d), const), _resident(w_all.shape, const),
             _resident(conv_w.shape, const), _resident(conv_p.shape, const)]
    if sample:
        args.append(state)
        specs.append(_resident(state.shape, lambda i: (0, 0, 0)))

    def out(width, dtype):
        return jax.ShapeDtypeStruct((m, width), dtype), pl.BlockSpec((tm, width), row)

    outs = [out(c_conv, BF16), out(n_heads * LANES, BF16), out(kvw, F32), out(kvw, F32),
            out(kvw, BF16), out(N_KV_HEADS * LANES, BF16), out(H_IDX * D_IDX, BF16), out(LANES, BF16),
            out(LANES, BF16), out(D_IDX, F32), out(H_IDX, F32)]
    if sample:
        outs.append(out(c_conv, F32))
        scratch = []
    else:
        outs.append((jax.ShapeDtypeStruct((n_seq, CONV_HALO, c_conv), F32),
                     pl.BlockSpec((1, CONV_HALO, c_conv), lambda i: (i // tiles_per_seq, 0, 0))))
        scratch = [pltpu.VMEM((tm + CONV_HALO, c_conv), F32), pltpu.VMEM((tm, c_conv), F32),
                   pltpu.VMEM((tm + CONV_HALO, LANES), F32)]

    return pl.pallas_call(
        functools.partial(_proj_kernel, tm=tm, tiles_per_seq=tiles_per_seq, sample=sample, d_mix=d_mix),
        grid=(nt,),
        in_specs=specs,
        out_specs=[o[1] for o in outs],
        out_shape=[o[0] for o in outs],
        scratch_shapes=scratch,
        compiler_params=_cparams(("arbitrary",)),
        name=name,
    )(*args)


def _key_to_f32(key):
    bits = key ^ ((key >> 31) & jnp.int32(0x7FFFFFFF))
    return lax.bitcast_convert_type(bits, F32)


def _chunk_loop(n, body, carry, unroll=2):
    assert unroll & (unroll - 1) == 0
    if isinstance(n, int) and n <= unroll:
        for c in range(n):
            carry = body(c, carry)
        return carry

    def run(first, count, carry):
        for u in range(count):
            carry = body(first + u, carry)
        return carry

    carry = lax.fori_loop(0, n // unroll, lambda i, c: run(i * unroll, unroll, c), carry)
    done = n // unroll * unroll
    part = unroll // 2
    while part >= 1:
        carry = lax.cond((n - done) // part % 2 == 1, lambda c, d=done, p=part: run(d, p, c), lambda c: c, carry)
        done = done + jnp.where((n - done) // part % 2 == 1, part, 0)
        part //= 2
    return carry


def _count(sc_ref, nchunks, ck, pred, whole=False):
    rows = sc_ref.shape[0]

    def body(c, acc):
        start = pl.multiple_of(c * ck, ck)
        hit = jnp.where(pred(sc_ref[:, pl.ds(start, ck)], start), 1.0, 0.0)
        part = hit[:, 0:LANES]
        for i in range(1, ck // LANES):
            part = part + hit[:, i * LANES:(i + 1) * LANES]
        return acc + part

    acc = _chunk_loop(nchunks, body, jnp.zeros((rows, LANES), F32), unroll=COUNT_UNROLL)
    per_row = jnp.sum(acc, axis=1, keepdims=True)
    return jnp.sum(per_row, axis=0, keepdims=True) if whole else per_row


def _position(shape, start, whole):
    col = start + lax.broadcasted_iota(I32, shape, 1)
    return lax.broadcasted_iota(I32, shape, 0) * shape[1] + col if whole else col


def _exact_threshold(sc_ref, nchunks, ck, n_sel, n_adm, total_cols, whole=False):
    rows = 1 if whole else sc_ref.shape[0]
    want = jnp.float32(n_sel)

    def bisect(i, carry):
        t, ct = carry
        cand = t + lax.shift_left(jnp.int32(1), 31 - i)
        thr = _key_to_f32(cand)
        cnt = _count(sc_ref, nchunks, ck, lambda blk, _: blk >= thr, whole)
        take = cnt >= want
        return jnp.where(take, cand, t), jnp.where(take, cnt, ct)

    t, ct = lax.fori_loop(0, 32, bisect, (jnp.full((rows, 1), INT_MIN, I32), jnp.zeros((rows, 1), F32)))
    full = n_adm <= n_sel
    thr = jnp.where(full, F32_LOWEST, _key_to_f32(t))
    tied = jnp.logical_and(jnp.logical_not(full), ct > want)

    @pl.when(jnp.sum(jnp.where(tied, 1.0, 0.0)) > 0.0)
    def _():
        n_gt = _count(sc_ref, nchunks, ck, lambda blk, _: blk > thr, whole)
        room = want - n_gt
        nbits = max(int(total_cols - 1).bit_length(), 1)

        def search(i, q):
            cand = q + lax.shift_left(jnp.int32(1), nbits - 1 - i)
            below = _count(sc_ref, nchunks, ck,
                           lambda blk, s: jnp.logical_and(blk == thr, _position(blk.shape, s, whole) < cand), whole)
            return jnp.where(below < room, cand, q)

        last = lax.fori_loop(0, nbits, search, jnp.zeros((rows, 1), I32))

        def drop(c, _):
            start = pl.multiple_of(c * ck, ck)
            blk = sc_ref[:, pl.ds(start, ck)]
            lose = jnp.logical_and(tied, jnp.logical_and(blk == thr, _position(blk.shape, start, whole) > last))
            sc_ref[:, pl.ds(start, ck)] = jnp.where(lose, -jnp.inf, blk)
            return 0

        lax.fori_loop(0, nchunks, drop, 0)

    return thr


def _select_threshold(sc_ref, thr_ref, nchunks, ck, n_sel, n_adm, total_cols, row_lo, row_hi, whole=False):
    rows = 1 if whole else sc_ref.shape[0]
    want = jnp.float32(n_sel)
    full = n_adm <= n_sel

    def unresolved(done):
        return jnp.sum(done) < rows

    def cond(state):
        it, go = state[0], state[1]
        return jnp.logical_and(it < SEARCH_CAP, go)

    def body(state):
        it, _, lo, hi, thr, done = state
        mid = lo + (hi - lo) * 0.5
        cnt = _count(sc_ref, nchunks, ck, lambda blk, _: blk >= mid, whole)
        go = unresolved(done)
        found = cnt == want
        thr = jnp.where(jnp.logical_and(found, done == 0.0), mid, thr)
        done = jnp.where(found, 1.0, done)
        return it + 1, go, jnp.where(cnt > want, mid, lo), jnp.where(cnt < want, mid, hi), thr, done

    done0 = jnp.where(full, 1.0, 0.0)
    init = (jnp.int32(0), unresolved(done0), row_lo, row_hi, jnp.where(full, F32_LOWEST, row_lo), done0)
    _, _, _, _, thr, done = lax.while_loop(cond, body, init)
    thr_ref[...] = thr

    @pl.when(unresolved(done))
    def _():
        exact = _exact_threshold(sc_ref, nchunks, ck, n_sel, n_adm, total_cols, whole)
        thr_ref[...] = jnp.where(done > 0.0, thr, exact)

    return thr_ref[...]


def _attn_kernel(qi_ref, w_ref, kia_ref, kib_ref, q_ref, k_ref, v_ref, o_ref, sc_ref, thr_ref, s_ref, wb_ref,
                 *, tq, th, n_sel, seq_len):
    t0 = pl.program_id(1) * tq
    nk = t0 // ATT_CK + 1
    row_pos = t0 + lax.broadcasted_iota(I32, (tq, 1), 0)
    n_heads = q_ref.shape[2] // LANES
    group = n_heads // N_KV_HEADS

    def fold(op, acc, x):
        for i in range(x.shape[1] // LANES):
            acc = op(acc, x[:, i * LANES:(i + 1) * LANES])
        return acc

    for h in range(H_IDX):
        wb_ref[h] = jnp.broadcast_to(w_ref[0, :, h:h + 1], (tq, LANES))

    def head_weight(h):
        return jnp.concatenate([wb_ref[h]] * (ATT_CK_SCORE // LANES), axis=1)

    def score(c, carry):
        hi_run, lo_run = carry
        start = pl.multiple_of(c * ATT_CK_SCORE, ATT_CK_SCORE)
        ka = kia_ref[0, pl.ds(start, ATT_CK_SCORE), :]
        kb = kib_ref[0, pl.ds(start, ATT_CK_SCORE), :]
        acc = jnp.zeros((tq, ATT_CK_SCORE), F32)
        for j in range(H_IDX // 2):
            pair = qi_ref[0, :, j * LANES:(j + 1) * LANES]
            acc = acc + jnp.maximum(_dot_nt(pair, ka), 0.0) * head_weight(2 * j)
            acc = acc + jnp.maximum(_dot_nt(pair, kb), 0.0) * head_weight(2 * j + 1)
        admissible = start + lax.broadcasted_iota(I32, (tq, ATT_CK_SCORE), 1) <= row_pos
        masked = jnp.where(admissible, acc, -jnp.inf)
        sc_ref[:, pl.ds(start, ATT_CK_SCORE)] = masked
        return fold(jnp.maximum, hi_run, masked), fold(jnp.minimum, lo_run, jnp.where(admissible, acc, jnp.inf))

    hi_run, lo_run = _chunk_loop(nk * (ATT_CK // ATT_CK_SCORE), score,
                                 (jnp.full((tq, LANES), -jnp.inf, F32), jnp.full((tq, LANES), jnp.inf, F32)),
                                 unroll=SCORE_UNROLL)

    thr = _select_threshold(sc_ref, thr_ref, nk, ATT_CK, n_sel, row_pos + 1, seq_len,
                            jnp.min(lo_run, axis=1, keepdims=True), jnp.max(hi_run, axis=1, keepdims=True))

    def to_mask(c, _):
        start = pl.multiple_of(c * ATT_CK, ATT_CK)
        sc_ref[:, pl.ds(start, ATT_CK)] = jnp.where(sc_ref[:, pl.ds(start, ATT_CK)] >= thr, 0.0, NEG_BIG)
        return 0

    lax.fori_loop(0, nk, to_mask, 0)

    rows = group * th

    def attend(u, _):
        r0 = pl.multiple_of(u // N_KV_HEADS * th, th)
        n = u % N_KV_HEADS
        kv_lanes = pl.ds(pl.multiple_of(n // 2 * LANES, LANES), LANES)
        qg = jnp.concatenate(
            [q_ref[0, pl.ds(r0, th), pl.ds(pl.multiple_of((group * n + g) * LANES, LANES), LANES)]
             for g in range(group)], axis=0)

        def logits(c, m_run):
            start = pl.multiple_of(c * ATT_CK, ATT_CK)
            s = (_dot_nt(qg, k_ref[0, pl.ds(start, ATT_CK), kv_lanes])
                 + jnp.concatenate([sc_ref[pl.ds(r0, th), pl.ds(start, ATT_CK)]] * group, axis=0))
            s_ref[:, pl.ds(start, ATT_CK)] = s
            return fold(jnp.maximum, m_run, s)

        m_run = _chunk_loop(nk, logits, jnp.full((rows, LANES), -jnp.inf, F32), unroll=ATT_UNROLL)
        m = jnp.max(m_run, axis=1, keepdims=True)

        v_lanes = pl.ds(pl.multiple_of(n * LANES, LANES), LANES)

        def weigh(c, acc):
            start = pl.multiple_of(c * ATT_CK, ATT_CK)
            p = jnp.exp(s_ref[:, pl.ds(start, ATT_CK)] - m)
            return acc + _dot(p.astype(BF16), v_ref[0, pl.ds(start, ATT_CK), v_lanes])

        acc = _chunk_loop(nk, weigh, jnp.zeros((rows, LANES), F32), unroll=ATT_UNROLL)
        out = acc[:, :HEAD_DIM] / acc[:, HEAD_DIM:]
        width = group * HEAD_DIM
        o_ref[0, pl.ds(r0, th), pl.ds(pl.multiple_of(n * width, width), width)] = jnp.concatenate(
            [out[g * th:(g + 1) * th] for g in range(group)], axis=1).astype(o_ref.dtype)
        return 0

    lax.fori_loop(0, tq // th * N_KV_HEADS, attend, 0)


def _prompt_attention(qi, wi, kia, kib, qpad, kb, vb, *, n_sel):
    b, s, _ = qi.shape
    tq = min(ATT_TQ, s)
    th = min(ATT_TH, tq)
    assert s % ATT_CK == 0 and ATT_CK % tq == 0 and ATT_CK % ATT_CK_SCORE == 0 and tq % th == 0
    n_heads = qpad.shape[2] // LANES
    tile = lambda bi, i: (bi, i, 0)
    seq = lambda bi, i: (bi, 0, 0)
    return pl.pallas_call(
        functools.partial(_attn_kernel, tq=tq, th=th, n_sel=n_sel, seq_len=s),
        grid=(b, s // tq),
        in_specs=[pl.BlockSpec((1, tq, qi.shape[2]), tile), pl.BlockSpec((1, tq, wi.shape[2]), tile),
                  _resident((1, s, LANES), seq), _resident((1, s, LANES), seq),
                  pl.BlockSpec((1, tq, qpad.shape[2]), tile),
                  _resident((1, s, kb.shape[2]), seq), _resident((1, s, vb.shape[2]), seq)],
        out_specs=pl.BlockSpec((1, tq, n_heads * HEAD_DIM), tile),
        out_shape=jax.ShapeDtypeStruct((b, s, n_heads * HEAD_DIM), BF16),
        scratch_shapes=[pltpu.VMEM((tq, s), F32), pltpu.VMEM((tq, 1), F32),
                        pltpu.VMEM((n_heads // N_KV_HEADS * th, s), F32), pltpu.VMEM((H_IDX, tq, LANES), F32)],
        compiler_params=_cparams(("arbitrary", "arbitrary")),
        name="prompt_attention",
    )(qi, wi, kia, kib, qpad, kb, vb)


def _native_pages(cache):
    rank = cache.ndim
    t = jnp.transpose(cache, (0, 1) + tuple(range(3, rank)) + (2,))
    return t.reshape(-1, cache.shape[2])


def _slot0_page(x, page):
    return jnp.pad(x[:, :, None], ((0, 0), (0, 0), (0, page - 1)))


def _sample_select_kernel(pt_ref, qi_ref, w_ref, kin_ref, cache_ref, mask_ref, buf_ref, keys_ref, row_ref, sc_ref,
                          thr_ref, sem, *, layer, n_pool, n_pages, page, n_sel, rows):
    b = pl.program_id(0)
    past = n_pages * page
    total = rows * LANES

    def page_copy(sample, j):
        src = pl.ds((layer * n_pool + pt_ref[sample, j]) * D_IDX, D_IDX)
        return pltpu.make_async_copy(cache_ref.at[src], buf_ref.at[:, pl.ds(j * page, page)], sem)

    def pages(sample, wait):
        def one(j, _):
            cp = page_copy(sample, j)
            cp.wait() if wait else cp.start()
            return 0

        lax.fori_loop(0, n_pages, one, 0)

    @pl.when(b == 0)
    def _():
        pages(0, wait=False)
        buf_ref[:, past + page:] = jnp.zeros((D_IDX, total - past - page), F32)

    buf_ref[:, past:past + page] = kin_ref[0]
    pages(b, wait=True)
    keys_ref[...] = buf_ref[...].astype(BF16)

    @pl.when(b + 1 < pl.num_programs(0))
    def _():
        pages(b + 1, wait=False)

    x = _dot(qi_ref[0], keys_ref[...])
    score = jnp.sum(jnp.maximum(x, 0.0) * w_ref[0], axis=0, keepdims=True)
    admissible = lax.broadcasted_iota(I32, (1, total), 1) <= past
    masked = jnp.where(admissible, score, -jnp.inf)
    row_ref[...] = masked
    for r in range(rows):
        sc_ref[r:r + 1, :] = row_ref[:, r * LANES:(r + 1) * LANES]
    thr = _select_threshold(sc_ref, thr_ref, 1, LANES, n_sel, jnp.full((1, 1), past + 1, I32), total,
                            jnp.min(jnp.where(admissible, score, jnp.inf), axis=1, keepdims=True),
                            jnp.max(masked, axis=1, keepdims=True), whole=True)
    mask_ref[0] = jnp.where(sc_ref[...] >= thr, 0.0, NEG_BIG)


def _sample_select(page_table, qi, wi, ki_new, cache_idx, *, layer, n_sel):
    db, n_pages = page_table.shape
    depth, n_pool, page, _ = cache_idx.shape
    assert page == LANES
    past = n_pages * page
    total = _round_up(past + page, SUBLANES * LANES)
    rows = total // LANES
    grid_spec = pltpu.PrefetchScalarGridSpec(
        num_scalar_prefetch=1,
        grid=(db,),
        in_specs=[pl.BlockSpec((1, H_IDX, D_IDX), lambda b, pt: (b, 0, 0)),
                  pl.BlockSpec((1, H_IDX, 1), lambda b, pt: (b, 0, 0)),
                  pl.BlockSpec((1, D_IDX, page), lambda b, pt: (b, 0, 0)),
                  pl.BlockSpec(memory_space=pl.ANY)],
        out_specs=pl.BlockSpec((1, rows, LANES), lambda b, pt: (b, 0, 0)),
        scratch_shapes=[pltpu.VMEM((D_IDX, total), F32), pltpu.VMEM((D_IDX, total), BF16), pltpu.VMEM((1, total), F32),
                        pltpu.VMEM((rows, LANES), F32), pltpu.VMEM((1, 1), F32), pltpu.SemaphoreType.DMA(())],
    )
    return pl.pallas_call(
        functools.partial(_sample_select_kernel, layer=layer, n_pool=n_pool, n_pages=n_pages, page=page, n_sel=n_sel,
                          rows=rows),
        grid_spec=grid_spec,
        out_shape=jax.ShapeDtypeStruct((db, rows, LANES), F32),
        compiler_params=_cparams(("arbitrary",)),
        name="sample_select",
    )(page_table, qi, wi, _slot0_page(ki_new, page), _native_pages(cache_idx))


def _sample_attend_kernel(pt_ref, mask_ref, qt_ref, knew_ref, vnew_ref, ck_ref, cv_ref, o_ref,
                          kbuf, vbuf, s_ref, qb_ref, sem_k, sem_v, *, layer, n_pool, n_pages, page, rows):
    b = pl.program_id(0)
    n_heads = qt_ref.shape[2]
    group = n_heads // N_KV_HEADS
    kvw = N_KV_HEADS * HEAD_DIM
    live = n_pages + 1

    def fetch(cache_ref, new_ref, buf, sem, sample, wait):
        def one(src, j):
            cp = pltpu.make_async_copy(src, buf.at[pl.ds(j * kvw, kvw)], sem)
            cp.wait() if wait else cp.start()

        def cached(j, _):
            one(cache_ref.at[pl.ds((layer * n_pool + pt_ref[sample, j]) * kvw, kvw)], j)
            return 0

        lax.fori_loop(0, n_pages, cached, 0)
        one(new_ref.at[sample], n_pages)

    k_pages = functools.partial(fetch, ck_ref, knew_ref, kbuf, sem_k)
    v_pages = functools.partial(fetch, cv_ref, vnew_ref, vbuf, sem_v)
    more = b + 1 < pl.num_programs(0)

    @pl.when(b == 0)
    def _():
        k_pages(0, wait=False)
        v_pages(0, wait=False)

    for h in range(n_heads):
        qb_ref[h] = jnp.broadcast_to(qt_ref[0, :, h:h + 1], (HEAD_DIM, LANES))

    s_ref[:, live * page:] = jnp.full((n_heads, (rows - live) * page), NEG_BIG, F32)
    k_pages(b, wait=True)

    def logits(j, _):
        row0 = pl.multiple_of(j * kvw, kvw)
        col0 = pl.multiple_of(j * page, page)
        drop = mask_ref[0, pl.ds(j, 1), :]
        for n in range(N_KV_HEADS):
            k_n = kbuf[pl.ds(row0 + n * HEAD_DIM, HEAD_DIM), :]
            for g in range(group):
                h = n * group + g
                s_ref[h:h + 1, pl.ds(col0, page)] = jnp.sum(k_n * qb_ref[h], axis=0, keepdims=True) + drop
        return 0

    lax.fori_loop(0, live, logits, 0, unroll=PAGE_UNROLL)

    @pl.when(more)
    def _():
        k_pages(b + 1, wait=False)

    s = s_ref[...]
    p = jnp.exp(s - jnp.max(s, axis=1, keepdims=True))
    s_ref[...] = p / jnp.sum(p, axis=1, keepdims=True)
    v_pages(b, wait=True)

    for h in range(n_heads):
        def weigh(j, acc):
            rows_h = pl.ds(pl.multiple_of(j * kvw, kvw) + h // group * HEAD_DIM, HEAD_DIM)
            return acc + vbuf[rows_h, :] * s_ref[h:h + 1, pl.ds(pl.multiple_of(j * page, page), page)]

        acc = lax.fori_loop(0, live, weigh, jnp.zeros((HEAD_DIM, LANES), F32), unroll=PAGE_UNROLL)
        o_ref[0, :, h:h + 1] = jnp.sum(acc, axis=1, keepdims=True)

    @pl.when(more)
    def _():
        v_pages(b + 1, wait=False)


def _sample_attend(page_table, mask, q, k_new, v_new, cache_k, cache_v, *, layer):
    db, n_pages = page_table.shape
    depth, n_pool, page = cache_k.shape[:3]
    n_heads = q.shape[1]
    rows = mask.shape[1]
    kvw = N_KV_HEADS * HEAD_DIM
    assert page == LANES and rows > n_pages
    grid_spec = pltpu.PrefetchScalarGridSpec(
        num_scalar_prefetch=1,
        grid=(db,),
        in_specs=[pl.BlockSpec((1, rows, LANES), lambda b, pt: (b, 0, 0)),
                  pl.BlockSpec((1, HEAD_DIM, n_heads), lambda b, pt: (b, 0, 0)),
                  pl.BlockSpec(memory_space=pl.ANY), pl.BlockSpec(memory_space=pl.ANY),
                  pl.BlockSpec(memory_space=pl.ANY), pl.BlockSpec(memory_space=pl.ANY)],
        out_specs=pl.BlockSpec((1, HEAD_DIM, n_heads), lambda b, pt: (b, 0, 0)),
        scratch_shapes=[pltpu.VMEM(((n_pages + 1) * kvw, page), F32), pltpu.VMEM(((n_pages + 1) * kvw, page), F32),
                        pltpu.VMEM((n_heads, rows * LANES), F32), pltpu.VMEM((n_heads, HEAD_DIM, LANES), F32),
                        pltpu.SemaphoreType.DMA(()), pltpu.SemaphoreType.DMA(())],
    )
    o_t = pl.pallas_call(
        functools.partial(_sample_attend_kernel, layer=layer, n_pool=n_pool, n_pages=n_pages, page=page, rows=rows),
        grid_spec=grid_spec,
        out_shape=jax.ShapeDtypeStruct((db, HEAD_DIM, n_heads), F32),
        compiler_params=_cparams(("arbitrary",)),
        name="sample_attend",
    )(page_table, mask, jnp.swapaxes(q, 1, 2).astype(F32), _slot0_page(k_new, page), _slot0_page(v_new, page),
      _native_pages(cache_k), _native_pages(cache_v))
    return jnp.swapaxes(o_t, 1, 2).reshape(db, n_heads * HEAD_DIM)


def _pack_ffn(w_in, w_out):
    d, f2 = w_in.shape
    f = f2 // 2
    fp = _round_up(f, FFN_TF)
    zc = jnp.zeros((d, fp - f), BF16)
    zr = jnp.zeros((fp - f, w_out.shape[1]), BF16)
    return (jnp.concatenate([w_in[:, :f].astype(BF16), zc], axis=1),
            jnp.concatenate([w_in[:, f:].astype(BF16), zc], axis=1),
            jnp.concatenate([w_out.astype(BF16), zr], axis=0))


def _pack_proj(w_in, d_mix):
    d = w_in.shape[0]
    c_conv = d_mix // C_CONV_FRACTION
    d_attn = d_mix - c_conv
    n_heads = d_attn // HEAD_DIM
    kvw = N_KV_HEADS * HEAD_DIM
    sizes = [2 * c_conv, d_attn, kvw, kvw, H_IDX * D_IDX, D_IDX, H_IDX]
    w_in = w_in.astype(BF16)
    parts, o = [], 0
    for sz in sizes:
        parts.append(w_in[:, o:o + sz])
        o += sz
    p_conv, q, k, v, qi, ki, wi = parts
    qh = q.reshape(d, n_heads, HEAD_DIM)
    zero = jnp.zeros_like(qh)
    odd = ((jnp.arange(n_heads) // (n_heads // N_KV_HEADS)) % 2 == 1)[None, :, None]
    qpad = jnp.concatenate([jnp.where(odd, zero, qh), jnp.where(odd, qh, zero)], axis=-1).reshape(d, n_heads * LANES)
    z = jnp.zeros((d, LANES - D_IDX), w_in.dtype)
    cols = [p_conv[:, :c_conv], p_conv[:, c_conv:], qpad, k, v, qi,
            jnp.concatenate([ki, z], axis=1), jnp.concatenate([z, ki], axis=1),
            jnp.concatenate([wi, jnp.zeros((d, LANES - H_IDX), w_in.dtype)], axis=1)]
    w_all = jnp.concatenate(cols, axis=1)
    assert w_all.shape[1] == _proj_layout(d_mix)[1]
    return w_all


def _own_half(o_pad, n_heads):
    odd = ((jnp.arange(n_heads) // (n_heads // N_KV_HEADS)) % 2 == 1)[:, None]
    return jnp.where(odd, o_pad[..., HEAD_DIM:], o_pad[..., :HEAD_DIM])


def kernel(x_prompt, x_sample, cache_k, cache_v, cache_idx_k, state_conv, page_table, norm_ffn1, ffn1_w_in, ffn1_w_out, norm_mix, w_in, conv_w, conv_b, conv_ln_g, conv_ln_b, w_out, norm_ffn2, ffn2_w_in, ffn2_w_out, norm_final):
    depth = w_in.shape[0]
    bsz, seq, d = x_prompt.shape
    db, dseq, _ = x_sample.shape
    assert dseq == 1
    d_mix = d
    c_conv = d_mix // C_CONV_FRACTION
    n_heads = (d_mix - c_conv) // HEAD_DIM
    page = cache_k.shape[2]
    n_pages = page_table.shape[1]
    past = n_pages * page
    n_sel_p = min(TOPK_MAX, seq // 4)
    n_sel_s = min(TOPK_MAX, (past + dseq) // 4)

    xp = x_prompt.reshape(bsz * seq, d)
    xs = x_sample.reshape(db, d)
    g_final = norm_final.reshape(1, d)
    outs_p, outs_s = [], []

    for l in range(depth):
        f1 = _pack_ffn(ffn1_w_in[l], ffn1_w_out[l])
        f2 = _pack_ffn(ffn2_w_in[l], ffn2_w_out[l])
        w_all = _pack_proj(w_in[l], d_mix)
        conv_p = jnp.stack([conv_b[l], conv_ln_g[l], conv_ln_b[l]])
        wo = w_out[l].astype(BF16)
        woc, woa = wo[:c_conv], wo[c_conv:]
        g1, gm, g2 = norm_ffn1[l].reshape(1, d), norm_mix[l].reshape(1, d), norm_ffn2[l].reshape(1, d)
        last = g_final if l == depth - 1 else None

        xp = _ffn(xp, g1, *f1, name="ffn1_prompt")
        (conv_o, qpad, k_new, v_new, kb, vb, qi, kia, kib, ki_new, wi, u_tail) = _proj(
            xp, gm, w_all, conv_w[l], conv_p, seq_len=seq, name="proj_prompt")
        r3 = lambda a: a.reshape(bsz, seq, a.shape[-1])
        attn = _prompt_attention(r3(qi), r3(wi), r3(kia), r3(kib), r3(qpad), r3(kb), r3(vb), n_sel=n_sel_p)
        xp = _ffn(xp, g2, *f2, mix=(conv_o, attn.reshape(bsz * seq, -1), woc, woa), g_final=last,
                  name="ffn2_prompt")
        outs_p.append((k_new.reshape(bsz, seq, N_KV_HEADS, HEAD_DIM), v_new.reshape(bsz, seq, N_KV_HEADS, HEAD_DIM),
                       ki_new.reshape(bsz, seq, D_IDX), u_tail[:, CONV_HALO - (CONV_WIDTH - 1):, :]))

        state = state_conv[l].astype(F32)
        xs = _ffn(xs, g1, *f1, name="ffn1_sample")
        (conv_o, qpad, k_new, v_new, _, _, qi, _, _, ki_new, wi, u_new) = _proj(
            xs, gm, w_all, conv_w[l], conv_p, seq_len=1, state=jnp.swapaxes(state, 0, 1), name="proj_sample")
        mask = _sample_select(page_table, qi.reshape(db, H_IDX, D_IDX), wi.reshape(db, H_IDX, 1), ki_new,
                              cache_idx_k, layer=l, n_sel=n_sel_s)
        attn = _sample_attend(page_table, mask, _own_half(qpad.reshape(db, n_heads, LANES), n_heads), k_new, v_new,
                              cache_k, cache_v, layer=l).astype(BF16)
        xs = _ffn(xs, g2, *f2, mix=(conv_o, attn, woc, woa), g_final=last, name="ffn2_sample")
        outs_s.append((k_new.reshape(db, 1, N_KV_HEADS, HEAD_DIM), v_new.reshape(db, 1, N_KV_HEADS, HEAD_DIM),
                       ki_new.reshape(db, 1, D_IDX),
                       jnp.concatenate([state[:, 1:, :], u_new[:, None, :]], axis=1)))

    stack = lambda outs, i: jnp.stack([o[i] for o in outs])
    return (xp.reshape(bsz, seq, d), xs.reshape(db, 1, d),
            stack(outs_p, 0), stack(outs_p, 1), stack(outs_p, 2), stack(outs_p, 3),
            stack(outs_s, 0), stack(outs_s, 1), stack(outs_s, 2), stack(outs_s, 3))
```

```python
import functools

import jax
import jax.numpy as jnp
from jax import lax
from jax.experimental import pallas as pl
from jax.experimental.pallas import tpu as pltpu

F32 = jnp.float32
BF16 = jnp.bfloat16
I32 = jnp.int32

C_CONV_FRACTION = 2
CONV_WIDTH = 31
HEAD_DIM = 64
N_KV_HEADS = 4
H_IDX = 16
D_IDX = 64
TOPK_MAX = 256
EPS = 1e-6
ATTN_SCALE = HEAD_DIM ** -0.5
IDX_SCALE = (D_IDX ** -0.5) * (H_IDX ** -0.5)

LANES = 128
SUBLANES = 8
VMEM_LIMIT_BYTES = 56 * 1024 * 1024

FFN_TM = 512
FFN1_TM = 1024
FFN_TF = 512
PROJ_TM = 256
CONV_HALO = 32
ATT_TQ = 128
ATT_TH = 128
ATT_CK_SCORE = 256
ATT_CK = 512
ATT_UNROLL = 8
SCORE_UNROLL = 4
COUNT_UNROLL = 1
SEARCH_CAP = 40
PAGE_UNROLL = 3

INT_MIN = -2 ** 31
NEG_BIG = -1e30
F32_LOWEST = float(jnp.finfo(jnp.float32).min)


def _round_up(x, m):
    return (x + m - 1) // m * m


def _cparams(sem):
    return pltpu.CompilerParams(dimension_semantics=sem, vmem_limit_bytes=VMEM_LIMIT_BYTES)


def _resident(shape, index_map):
    return pl.BlockSpec(shape, index_map, pipeline_mode=pl.Buffered(1))


def _rms(x, g):
    ms = jnp.mean(x * x, axis=-1, keepdims=True)
    return x * lax.rsqrt(ms + EPS) * g


def _dot(a, b):
    return jnp.dot(a, b, preferred_element_type=F32)


def _dot_nt(a, b):
    return lax.dot_general(a, b, (((1,), (1,)), ((), ())), preferred_element_type=F32)


def _ffn_kernel(*refs, has_mix, has_final):
    it = iter(refs)
    x_ref = next(it)
    if has_mix:
        mc_ref, ma_ref, woc_ref, woa_ref = next(it), next(it), next(it), next(it)
    g_ref, wa_ref, wb_ref, wo_ref = next(it), next(it), next(it), next(it)
    if has_final:
        gf_ref = next(it)
    o_ref = next(it)
    xn_ref = next(it)

    f = pl.program_id(1)

    @pl.when(f == 0)
    def _():
        x = x_ref[...]
        if has_mix:
            x = x + _dot(mc_ref[...], woc_ref[...]) + _dot(ma_ref[...], woa_ref[...])
        o_ref[...] = x
        xn_ref[...] = _rms(x, g_ref[...]).astype(BF16)

    xn = xn_ref[...]
    a = _dot(xn, wa_ref[...])
    b = _dot(xn, wb_ref[...])
    act = (a * jax.nn.sigmoid(a) * b).astype(BF16)
    o_ref[...] += 0.5 * _dot(act, wo_ref[...])

    if has_final:
        @pl.when(f == pl.num_programs(1) - 1)
        def _():
            o_ref[...] = _rms(o_ref[...], gf_ref[...])


def _ffn(x, g, wa, wb, wo, *, mix=None, g_final=None, name, tm_max=FFN_TM):
    m, d = x.shape
    fp = wa.shape[1]
    tm = min(tm_max, m)
    assert m % tm == 0 and fp % FFN_TF == 0
    nf = fp // FFN_TF
    has_mix = mix is not None
    has_final = g_final is not None

    row = lambda i, f: (i, 0)
    const = lambda i, f: (0, 0)
    args = [x]
    specs = [pl.BlockSpec((tm, d), row)]
    if has_mix:
        mc, ma, woc, woa = mix
        args += [mc, ma, woc, woa]
        specs += [pl.BlockSpec((tm, mc.shape[1]), row), pl.BlockSpec((tm, ma.shape[1]), row),
                  _resident(woc.shape, const), _resident(woa.shape, const)]
    args += [g, wa, wb, wo]
    specs += [_resident((1, d), const),
              pl.BlockSpec((d, FFN_TF), lambda i, f: (0, f)),
              pl.BlockSpec((d, FFN_TF), lambda i, f: (0, f)),
              pl.BlockSpec((FFN_TF, d), lambda i, f: (f, 0))]
    if has_final:
        args.append(g_final)
        specs.append(_resident((1, d), const))

    return pl.pallas_call(
        functools.partial(_ffn_kernel, has_mix=has_mix, has_final=has_final),
        grid=(m // tm, nf),
        in_specs=specs,
        out_specs=pl.BlockSpec((tm, d), row),
        out_shape=jax.ShapeDtypeStruct((m, d), F32),
        scratch_shapes=[pltpu.VMEM((tm, d), BF16)],
        compiler_params=_cparams(("arbitrary", "arbitrary")),
        name=name,
    )(*args)


def _proj_layout(d_mix):
    c_conv = d_mix // C_CONV_FRACTION
    d_attn = d_mix - c_conv
    n_heads = d_attn // HEAD_DIM
    widths = dict(ca=c_conv, cg=c_conv, qpad=n_heads * LANES, k=N_KV_HEADS * HEAD_DIM,
                  v=N_KV_HEADS * HEAD_DIM, qi=H_IDX * D_IDX, kia=LANES, kib=LANES, wi=LANES)
    off, o = {}, 0
    for name, w in widths.items():
        off[name] = (o, o + w)
        o += w
    return off, o


def _proj_kernel(*refs, tm, tiles_per_seq, sample, d_mix):
    it = iter(refs)
    x_ref, g_ref, w_ref, cw_ref, cp_ref = next(it), next(it), next(it), next(it), next(it)
    if sample:
        st_ref = next(it)
    (conv_ref, qpad_ref, k_ref, v_ref, kb_ref, vb_ref, qi_ref, kia_ref, kib_ref, ki_ref, wi_ref,
     u_ref) = (next(it) for _ in range(12))
    if not sample:
        win_ref, y_ref, z_ref = next(it), next(it), next(it)

    off, _ = _proj_layout(d_mix)
    c_conv = d_mix // C_CONV_FRACTION

    def cols(name):
        lo, hi = off[name]
        return w_ref[:, lo:hi]

    if not sample:
        @pl.when(pl.program_id(0) % tiles_per_seq == 0)
        def _():
            win_ref[0:CONV_HALO, :] = jnp.zeros((CONV_HALO, c_conv), F32)

    xn = _rms(x_ref[...], g_ref[...]).astype(BF16)
    u = _dot(xn, cols("ca")) * jax.nn.sigmoid(_dot(xn, cols("cg")))

    bias = cp_ref[0:1, :]
    ln_g = cp_ref[1:2, :]
    ln_b = cp_ref[2:3, :]

    if sample:
        u_ref[...] = u
        y = bias + cw_ref[CONV_WIDTH - 1:CONV_WIDTH, :] * u
        for j in range(CONV_WIDTH - 1):
            y = y + cw_ref[j:j + 1, :] * st_ref[j]
    else:
        win_ref[CONV_HALO:CONV_HALO + tm, :] = u
        first = CONV_HALO - (CONV_WIDTH - 1)
        for c in range(c_conv // LANES):
            cs = slice(c * LANES, (c + 1) * LANES)
            acc = jnp.zeros((tm, LANES), F32) + bias[:, cs]
            for r in range(SUBLANES):
                taps = [j for j in range(CONV_WIDTH) if (first + j) % SUBLANES == r]
                if taps:
                    base = first + taps[0]
                    span = first + taps[-1] + tm - base
                    if r:
                        z_ref[0:span, :] = win_ref[base:base + span, cs]
                    for j in taps:
                        lo = first + j - base
                        z = z_ref[lo:lo + tm, :] if r else win_ref[first + j:first + j + tm, cs]
                        acc = acc + cw_ref[j:j + 1, cs] * z
            y_ref[:, cs] = acc
        y = y_ref[...]
        tail = win_ref[tm:tm + CONV_HALO, :]
        u_ref[0] = tail
        win_ref[0:CONV_HALO, :] = tail

    mu = jnp.mean(y, axis=-1, keepdims=True)
    var = jnp.mean(jnp.square(y - mu), axis=-1, keepdims=True)
    yn = (y - mu) * lax.rsqrt(var + EPS) * ln_g + ln_b
    conv_ref[...] = (yn * jax.nn.sigmoid(yn)).astype(BF16)

    qpad_ref[...] = (_dot(xn, cols("qpad")) * ATTN_SCALE).astype(BF16)
    kk = _dot(xn, cols("k"))
    vv = _dot(xn, cols("v"))
    k_ref[...] = kk
    v_ref[...] = vv
    kb_ref[...] = kk.astype(BF16)
    ones = jnp.ones((vv.shape[0], LANES - HEAD_DIM), F32)
    vb_ref[...] = jnp.concatenate(
        [t for n in range(N_KV_HEADS) for t in (vv[:, n * HEAD_DIM:(n + 1) * HEAD_DIM], ones)], axis=1).astype(BF16)
    qi_ref[...] = _dot(xn, cols("qi")).astype(BF16)
    kia = _dot(xn, cols("kia"))
    kia_ref[...] = kia.astype(BF16)
    kib_ref[...] = _dot(xn, cols("kib")).astype(BF16)
    ki_ref[...] = kia[:, :D_IDX]
    wi_ref[...] = _dot(xn, cols("wi"))[:, :H_IDX] * IDX_SCALE


def _proj(x, g, w_all, conv_w, conv_p, *, seq_len, state=None, name):
    m, d = x.shape
    d_mix = d
    c_conv = d_mix // C_CONV_FRACTION
    n_heads = (d_mix - c_conv) // HEAD_DIM
    kvw = N_KV_HEADS * HEAD_DIM
    sample = state is not None
    tm = m if sample else min(PROJ_TM, seq_len)
    assert m % tm == 0 and seq_len % tm == 0 or sample
    assert tm >= CONV_HALO or sample
    nt = m // tm
    tiles_per_seq = max(seq_len // tm, 1)
    n_seq = m // seq_len

    row = lambda i: (i, 0)
    const = lambda i: (0, 0)
    args = [x, g, w_all, conv_w, conv_p]
    specs = [pl.BlockSpec((tm, d), row), _resident((1, d), const), _resident(w_all.shape, const),
             _resident(conv_w.shape, const), _resident(conv_p.shape, const)]
    if sample:
        args.append(state)
        specs.append(_resident(state.shape, lambda i: (0, 0, 0)))

    def out(width, dtype):
        return jax.ShapeDtypeStruct((m, width), dtype), pl.BlockSpec((tm, width), row)

    outs = [out(c_conv, BF16), out(n_heads * LANES, BF16), out(kvw, F32), out(kvw, F32),
            out(kvw, BF16), out(N_KV_HEADS * LANES, BF16), out(H_IDX * D_IDX, BF16), out(LANES, BF16),
            out(LANES, BF16), out(D_IDX, F32), out(H_IDX, F32)]
    if sample:
        outs.append(out(c_conv, F32))
        scratch = []
    else:
        outs.append((jax.ShapeDtypeStruct((n_seq, CONV_HALO, c_conv), F32),
                     pl.BlockSpec((1, CONV_HALO, c_conv), lambda i: (i // tiles_per_seq, 0, 0))))
        scratch = [pltpu.VMEM((tm + CONV_HALO, c_conv), F32), pltpu.VMEM((tm, c_conv), F32),
                   pltpu.VMEM((tm + CONV_HALO, LANES), F32)]

    return pl.pallas_call(
        functools.partial(_proj_kernel, tm=tm, tiles_per_seq=tiles_per_seq, sample=sample, d_mix=d_mix),
        grid=(nt,),
        in_specs=specs,
        out_specs=[o[1] for o in outs],
        out_shape=[o[0] for o in outs],
        scratch_shapes=scratch,
        compiler_params=_cparams(("arbitrary",)),
        name=name,
    )(*args)


def _key_to_f32(key):
    bits = key ^ ((key >> 31) & jnp.int32(0x7FFFFFFF))
    return lax.bitcast_convert_type(bits, F32)


def _chunk_loop(n, body, carry, unroll=2):
    assert unroll & (unroll - 1) == 0
    if isinstance(n, int) and n <= unroll:
        for c in range(n):
            carry = body(c, carry)
        return carry

    def run(first, count, carry):
        for u in range(count):
            carry = body(first + u, carry)
        return carry

    carry = lax.fori_loop(0, n // unroll, lambda i, c: run(i * unroll, unroll, c), carry)
    done = n // unroll * unroll
    part = unroll // 2
    while part >= 1:
        carry = lax.cond((n - done) // part % 2 == 1, lambda c, d=done, p=part: run(d, p, c), lambda c: c, carry)
        done = done + jnp.where((n - done) // part % 2 == 1, part, 0)
        part //= 2
    return carry


def _count(sc_ref, nchunks, ck, pred, whole=False):
    rows = sc_ref.shape[0]

    def body(c, acc):
        start = pl.multiple_of(c * ck, ck)
        hit = jnp.where(pred(sc_ref[:, pl.ds(start, ck)], start), 1.0, 0.0)
        part = hit[:, 0:LANES]
        for i in range(1, ck // LANES):
            part = part + hit[:, i * LANES:(i + 1) * LANES]
        return acc + part

    acc = _chunk_loop(nchunks, body, jnp.zeros((rows, LANES), F32), unroll=COUNT_UNROLL)
    per_row = jnp.sum(acc, axis=1, keepdims=True)
    return jnp.sum(per_row, axis=0, keepdims=True) if whole else per_row


def _position(shape, start, whole):
    col = start + lax.broadcasted_iota(I32, shape, 1)
    return lax.broadcasted_iota(I32, shape, 0) * shape[1] + col if whole else col


def _exact_threshold(sc_ref, nchunks, ck, n_sel, n_adm, total_cols, whole=False):
    rows = 1 if whole else sc_ref.shape[0]
    want = jnp.float32(n_sel)

    def bisect(i, carry):
        t, ct = carry
        cand = t + lax.shift_left(jnp.int32(1), 31 - i)
        thr = _key_to_f32(cand)
        cnt = _count(sc_ref, nchunks, ck, lambda blk, _: blk >= thr, whole)
        take = cnt >= want
        return jnp.where(take, cand, t), jnp.where(take, cnt, ct)

    t, ct = lax.fori_loop(0, 32, bisect, (jnp.full((rows, 1), INT_MIN, I32), jnp.zeros((rows, 1), F32)))
    full = n_adm <= n_sel
    thr = jnp.where(full, F32_LOWEST, _key_to_f32(t))
    tied = jnp.logical_and(jnp.logical_not(full), ct > want)

    @pl.when(jnp.sum(jnp.where(tied, 1.0, 0.0)) > 0.0)
    def _():
        n_gt = _count(sc_ref, nchunks, ck, lambda blk, _: blk > thr, whole)
        room = want - n_gt
        nbits = max(int(total_cols - 1).bit_length(), 1)

        def search(i, q):
            cand = q + lax.shift_left(jnp.int32(1), nbits - 1 - i)
            below = _count(sc_ref, nchunks, ck,
                           lambda blk, s: jnp.logical_and(blk == thr, _position(blk.shape, s, whole) < cand), whole)
            return jnp.where(below < room, cand, q)

        last = lax.fori_loop(0, nbits, search, jnp.zeros((rows, 1), I32))

        def drop(c, _):
            start = pl.multiple_of(c * ck, ck)
            blk = sc_ref[:, pl.ds(start, ck)]
            lose = jnp.logical_and(tied, jnp.logical_and(blk == thr, _position(blk.shape, start, whole) > last))
            sc_ref[:, pl.ds(start, ck)] = jnp.where(lose, -jnp.inf, blk)
            return 0

        lax.fori_loop(0, nchunks, drop, 0)

    return thr


def _select_threshold(sc_ref, thr_ref, nchunks, ck, n_sel, n_adm, total_cols, row_lo, row_hi, whole=False):
    rows = 1 if whole else sc_ref.shape[0]
    want = jnp.float32(n_sel)
    full = n_adm <= n_sel

    def unresolved(done):
        return jnp.sum(done) < rows

    def cond(state):
        it, go = state[0], state[1]
        return jnp.logical_and(it < SEARCH_CAP, go)

    def body(state):
        it, _, lo, hi, thr, done = state
        mid = lo + (hi - lo) * 0.5
        cnt = _count(sc_ref, nchunks, ck, lambda blk, _: blk >= mid, whole)
        go = unresolved(done)
        found = cnt == want
        thr = jnp.where(jnp.logical_and(found, done == 0.0), mid, thr)
        done = jnp.where(found, 1.0, done)
        return it + 1, go, jnp.where(cnt > want, mid, lo), jnp.where(cnt < want, mid, hi), thr, done

    done0 = jnp.where(full, 1.0, 0.0)
    init = (jnp.int32(0), unresolved(done0), row_lo, row_hi, jnp.where(full, F32_LOWEST, row_lo), done0)
    _, _, _, _, thr, done = lax.while_loop(cond, body, init)
    thr_ref[...] = thr

    @pl.when(unresolved(done))
    def _():
        exact = _exact_threshold(sc_ref, nchunks, ck, n_sel, n_adm, total_cols, whole)
        thr_ref[...] = jnp.where(done > 0.0, thr, exact)

    return thr_ref[...]


def _attn_kernel(qi_ref, w_ref, kia_ref, kib_ref, q_ref, k_ref, v_ref, o_ref, sc_ref, thr_ref, s_ref, wb_ref,
                 *, tq, th, n_sel, seq_len):
    t0 = pl.program_id(1) * tq
    nk = t0 // ATT_CK + 1
    row_pos = t0 + lax.broadcasted_iota(I32, (tq, 1), 0)
    n_heads = q_ref.shape[2] // LANES
    group = n_heads // N_KV_HEADS

    def fold(op, acc, x):
        for i in range(x.shape[1] // LANES):
            acc = op(acc, x[:, i * LANES:(i + 1) * LANES])
        return acc

    for h in range(H_IDX):
        wb_ref[h] = jnp.broadcast_to(w_ref[0, :, h:h + 1], (tq, LANES))

    def head_weight(h):
        return jnp.concatenate([wb_ref[h]] * (ATT_CK_SCORE // LANES), axis=1)

    def score(c, carry):
        hi_run, lo_run = carry
        start = pl.multiple_of(c * ATT_CK_SCORE, ATT_CK_SCORE)
        ka = kia_ref[0, pl.ds(start, ATT_CK_SCORE), :]
        kb = kib_ref[0, pl.ds(start, ATT_CK_SCORE), :]
        acc = jnp.zeros((tq, ATT_CK_SCORE), F32)
        for j in range(H_IDX // 2):
            pair = qi_ref[0, :, j * LANES:(j + 1) * LANES]
            acc = acc + jnp.maximum(_dot_nt(pair, ka), 0.0) * head_weight(2 * j)
            acc = acc + jnp.maximum(_dot_nt(pair, kb), 0.0) * head_weight(2 * j + 1)
        admissible = start + lax.broadcasted_iota(I32, (tq, ATT_CK_SCORE), 1) <= row_pos
        masked = jnp.where(admissible, acc, -jnp.inf)
        sc_ref[:, pl.ds(start, ATT_CK_SCORE)] = masked
        return fold(jnp.maximum, hi_run, masked), fold(jnp.minimum, lo_run, jnp.where(admissible, acc, jnp.inf))

    hi_run, lo_run = _chunk_loop(nk * (ATT_CK // ATT_CK_SCORE), score,
                                 (jnp.full((tq, LANES), -jnp.inf, F32), jnp.full((tq, LANES), jnp.inf, F32)),
                                 unroll=SCORE_UNROLL)

    _select_threshold(sc_ref, thr_ref, nk, ATT_CK, n_sel, row_pos + 1, seq_len,
                      jnp.min(lo_run, axis=1, keepdims=True), jnp.max(hi_run, axis=1, keepdims=True))

    rows = group * th

    def attend(u, _):
        r0 = pl.multiple_of(u // N_KV_HEADS * th, th)
        n = u % N_KV_HEADS
        kv_lanes = pl.ds(pl.multiple_of(n // 2 * LANES, LANES), LANES)
        qg = jnp.concatenate(
            [q_ref[0, pl.ds(r0, th), pl.ds(pl.multiple_of((group * n + g) * LANES, LANES), LANES)]
             for g in range(group)], axis=0)

        thr = thr_ref[pl.ds(r0, th), :]

        def logits(c, m_run):
            start = pl.multiple_of(c * ATT_CK, ATT_CK)
            drop = jnp.where(sc_ref[pl.ds(r0, th), pl.ds(start, ATT_CK)] >= thr, 0.0, NEG_BIG)
            s = _dot_nt(qg, k_ref[0, pl.ds(start, ATT_CK), kv_lanes]) + jnp.concatenate([drop] * group, axis=0)
            s_ref[:, pl.ds(start, ATT_CK)] = s
            return fold(jnp.maximum, m_run, s)

        m_run = _chunk_loop(nk, logits, jnp.full((rows, LANES), -jnp.inf, F32), unroll=ATT_UNROLL)
        m = jnp.max(m_run, axis=1, keepdims=True)

        v_lanes = pl.ds(pl.multiple_of(n * LANES, LANES), LANES)

        def weigh(c, acc):
            start = pl.multiple_of(c * ATT_CK, ATT_CK)
            p = jnp.exp(s_ref[:, pl.ds(start, ATT_CK)] - m)
            return acc + _dot(p.astype(BF16), v_ref[0, pl.ds(start, ATT_CK), v_lanes])

        acc = _chunk_loop(nk, weigh, jnp.zeros((rows, LANES), F32), unroll=ATT_UNROLL)
        out = acc[:, :HEAD_DIM] / acc[:, HEAD_DIM:]
        width = group * HEAD_DIM
        o_ref[0, pl.ds(r0, th), pl.ds(pl.multiple_of(n * width, width), width)] = jnp.concatenate(
            [out[g * th:(g + 1) * th] for g in range(group)], axis=1).astype(o_ref.dtype)
        return 0

    lax.fori_loop(0, tq // th * N_KV_HEADS, attend, 0)


def _prompt_attention(qi, wi, kia, kib, qpad, kb, vb, *, n_sel):
    b, s, _ = qi.shape
    tq = min(ATT_TQ, s)
    th = min(ATT_TH, tq)
    assert s % ATT_CK == 0 and ATT_CK % tq == 0 and ATT_CK % ATT_CK_SCORE == 0 and tq % th == 0
    n_heads = qpad.shape[2] // LANES
    tile = lambda bi, i: (bi, i, 0)
    seq = lambda bi, i: (bi, 0, 0)
    return pl.pallas_call(
        functools.partial(_attn_kernel, tq=tq, th=th, n_sel=n_sel, seq_len=s),
        grid=(b, s // tq),
        in_specs=[pl.BlockSpec((1, tq, qi.shape[2]), tile), pl.BlockSpec((1, tq, wi.shape[2]), tile),
                  _resident((1, s, LANES), seq), _resident((1, s, LANES), seq),
                  pl.BlockSpec((1, tq, qpad.shape[2]), tile),
                  _resident((1, s, kb.shape[2]), seq), _resident((1, s, vb.shape[2]), seq)],
        out_specs=pl.BlockSpec((1, tq, n_heads * HEAD_DIM), tile),
        out_shape=jax.ShapeDtypeStruct((b, s, n_heads * HEAD_DIM), BF16),
        scratch_shapes=[pltpu.VMEM((tq, s), F32), pltpu.VMEM((tq, 1), F32),
                        pltpu.VMEM((n_heads // N_KV_HEADS * th, s), F32), pltpu.VMEM((H_IDX, tq, LANES), F32)],
        compiler_params=_cparams(("arbitrary", "arbitrary")),
        name="prompt_attention",
    )(qi, wi, kia, kib, qpad, kb, vb)


def _native_pages(cache):
    rank = cache.ndim
    t = jnp.transpose(cache, (0, 1) + tuple(range(3, rank)) + (2,))
    return t.reshape(-1, cache.shape[2])


def _slot0_page(x, page):
    return jnp.pad(x[:, :, None], ((0, 0), (0, 0), (0, page - 1)))


def _sample_select_kernel(pt_ref, qi_ref, w_ref, kin_ref, cache_ref, mask_ref, buf_ref, keys_ref, row_ref, sc_ref,
                          thr_ref, sem, *, layer, n_pool, n_pages, page, n_sel, rows):
    b = pl.program_id(0)
    past = n_pages * page
    total = rows * LANES

    def page_copy(sample, j):
        src = pl.ds((layer * n_pool + pt_ref[sample, j]) * D_IDX, D_IDX)
        return pltpu.make_async_copy(cache_ref.at[src], buf_ref.at[:, pl.ds(j * page, page)], sem)

    def pages(sample, wait):
        def one(j, _):
            cp = page_copy(sample, j)
            cp.wait() if wait else cp.start()
            return 0

        lax.fori_loop(0, n_pages, one, 0)

    @pl.when(b == 0)
    def _():
        pages(0, wait=False)
        buf_ref[:, past + page:] = jnp.zeros((D_IDX, total - past - page), F32)

    buf_ref[:, past:past + page] = kin_ref[0]
    pages(b, wait=True)
    keys_ref[...] = buf_ref[...].astype(BF16)

    @pl.when(b + 1 < pl.num_programs(0))
    def _():
        pages(b + 1, wait=False)

    x = _dot(qi_ref[0], keys_ref[...])
    score = jnp.sum(jnp.maximum(x, 0.0) * w_ref[0], axis=0, keepdims=True)
    admissible = lax.broadcasted_iota(I32, (1, total), 1) <= past
    masked = jnp.where(admissible, score, -jnp.inf)
    row_ref[...] = masked
    for r in range(rows):
        sc_ref[r:r + 1, :] = row_ref[:, r * LANES:(r + 1) * LANES]
    thr = _select_threshold(sc_ref, thr_ref, 1, LANES, n_sel, jnp.full((1, 1), past + 1, I32), total,
                            jnp.min(jnp.where(admissible, score, jnp.inf), axis=1, keepdims=True),
                            jnp.max(masked, axis=1, keepdims=True), whole=True)
    mask_ref[0] = jnp.where(sc_ref[...] >= thr, 0.0, NEG_BIG)


def _sample_select(page_table, qi, wi, ki_new, cache_idx, *, layer, n_sel):
    db, n_pages = page_table.shape
    depth, n_pool, page, _ = cache_idx.shape
    assert page == LANES
    past = n_pages * page
    total = _round_up(past + page, SUBLANES * LANES)
    rows = total // LANES
    grid_spec = pltpu.PrefetchScalarGridSpec(
        num_scalar_prefetch=1,
        grid=(db,),
        in_specs=[pl.BlockSpec((1, H_IDX, D_IDX), lambda b, pt: (b, 0, 0)),
                  pl.BlockSpec((1, H_IDX, 1), lambda b, pt: (b, 0, 0)),
                  pl.BlockSpec((1, D_IDX, page), lambda b, pt: (b, 0, 0)),
                  pl.BlockSpec(memory_space=pl.ANY)],
        out_specs=pl.BlockSpec((1, rows, LANES), lambda b, pt: (b, 0, 0)),
        scratch_shapes=[pltpu.VMEM((D_IDX, total), F32), pltpu.VMEM((D_IDX, total), BF16), pltpu.VMEM((1, total), F32),
                        pltpu.VMEM((rows, LANES), F32), pltpu.VMEM((1, 1), F32), pltpu.SemaphoreType.DMA(())],
    )
    return pl.pallas_call(
        functools.partial(_sample_select_kernel, layer=layer, n_pool=n_pool, n_pages=n_pages, page=page, n_sel=n_sel,
                          rows=rows),
        grid_spec=grid_spec,
        out_shape=jax.ShapeDtypeStruct((db, rows, LANES), F32),
        compiler_params=_cparams(("arbitrary",)),
        name="sample_select",
    )(page_table, qi, wi, _slot0_page(ki_new, page), _native_pages(cache_idx))


def _sample_attend_kernel(pt_ref, mask_ref, qt_ref, knew_ref, vnew_ref, ck_ref, cv_ref, o_ref,
                          kbuf, vbuf, s_ref, qb_ref, sem_k, sem_v, *, layer, n_pool, n_pages, page, rows):
    b = pl.program_id(0)
    n_heads = qt_ref.shape[2]
    group = n_heads // N_KV_HEADS
    kvw = N_KV_HEADS * HEAD_DIM
    live = n_pages + 1

    def fetch(cache_ref, new_ref, buf, sem, sample, wait):
        def one(src, j):
            cp = pltpu.make_async_copy(src, buf.at[pl.ds(j * kvw, kvw)], sem)
            cp.wait() if wait else cp.start()

        def cached(j, _):
            one(cache_ref.at[pl.ds((layer * n_pool + pt_ref[sample, j]) * kvw, kvw)], j)
            return 0

        lax.fori_loop(0, n_pages, cached, 0)
        one(new_ref.at[sample], n_pages)

    k_pages = functools.partial(fetch, ck_ref, knew_ref, kbuf, sem_k)
    v_pages = functools.partial(fetch, cv_ref, vnew_ref, vbuf, sem_v)
    more = b + 1 < pl.num_programs(0)

    @pl.when(b == 0)
    def _():
        k_pages(0, wait=False)
        v_pages(0, wait=False)

    for h in range(n_heads):
        qb_ref[h] = jnp.broadcast_to(qt_ref[0, :, h:h + 1], (HEAD_DIM, LANES))

    s_ref[:, live * page:] = jnp.full((n_heads, (rows - live) * page), NEG_BIG, F32)
    k_pages(b, wait=True)

    def logits(j, _):
        row0 = pl.multiple_of(j * kvw, kvw)
        col0 = pl.multiple_of(j * page, page)
        drop = mask_ref[0, pl.ds(j, 1), :]
        for n in range(N_KV_HEADS):
            k_n = kbuf[pl.ds(row0 + n * HEAD_DIM, HEAD_DIM), :]
            for g in range(group):
                h = n * group + g
                s_ref[h:h + 1, pl.ds(col0, page)] = jnp.sum(k_n * qb_ref[h], axis=0, keepdims=True) + drop
        return 0

    lax.fori_loop(0, live, logits, 0, unroll=PAGE_UNROLL)

    @pl.when(more)
    def _():
        k_pages(b + 1, wait=False)

    s = s_ref[...]
    p = jnp.exp(s - jnp.max(s, axis=1, keepdims=True))
    s_ref[...] = p / jnp.sum(p, axis=1, keepdims=True)
    v_pages(b, wait=True)

    for h in range(n_heads):
        def weigh(j, acc):
            rows_h = pl.ds(pl.multiple_of(j * kvw, kvw) + h // group * HEAD_DIM, HEAD_DIM)
            return acc + vbuf[rows_h, :] * s_ref[h:h + 1, pl.ds(pl.multiple_of(j * page, page), page)]

        acc = lax.fori_loop(0, live, weigh, jnp.zeros((HEAD_DIM, LANES), F32), unroll=PAGE_UNROLL)
        o_ref[0, :, h:h + 1] = jnp.sum(acc, axis=1, keepdims=True)

    @pl.when(more)
    def _():
        v_pages(b + 1, wait=False)


def _sample_attend(page_table, mask, q, k_new, v_new, cache_k, cache_v, *, layer):
    db, n_pages = page_table.shape
    depth, n_pool, page = cache_k.shape[:3]
    n_heads = q.shape[1]
    rows = mask.shape[1]
    kvw = N_KV_HEADS * HEAD_DIM
    assert page == LANES and rows > n_pages
    grid_spec = pltpu.PrefetchScalarGridSpec(
        num_scalar_prefetch=1,
        grid=(db,),
        in_specs=[pl.BlockSpec((1, rows, LANES), lambda b, pt: (b, 0, 0)),
                  pl.BlockSpec((1, HEAD_DIM, n_heads), lambda b, pt: (b, 0, 0)),
                  pl.BlockSpec(memory_space=pl.ANY), pl.BlockSpec(memory_space=pl.ANY),
                  pl.BlockSpec(memory_space=pl.ANY), pl.BlockSpec(memory_space=pl.ANY)],
        out_specs=pl.BlockSpec((1, HEAD_DIM, n_heads), lambda b, pt: (b, 0, 0)),
        scratch_shapes=[pltpu.VMEM(((n_pages + 1) * kvw, page), F32), pltpu.VMEM(((n_pages + 1) * kvw, page), F32),
                        pltpu.VMEM((n_heads, rows * LANES), F32), pltpu.VMEM((n_heads, HEAD_DIM, LANES), F32),
                        pltpu.SemaphoreType.DMA(()), pltpu.SemaphoreType.DMA(())],
    )
    o_t = pl.pallas_call(
        functools.partial(_sample_attend_kernel, layer=layer, n_pool=n_pool, n_pages=n_pages, page=page, rows=rows),
        grid_spec=grid_spec,
        out_shape=jax.ShapeDtypeStruct((db, HEAD_DIM, n_heads), F32),
        compiler_params=_cparams(("arbitrary",)),
        name="sample_attend",
    )(page_table, mask, jnp.swapaxes(q, 1, 2).astype(F32), _slot0_page(k_new, page), _slot0_page(v_new, page),
      _native_pages(cache_k), _native_pages(cache_v))
    return jnp.swapaxes(o_t, 1, 2).reshape(db, n_heads * HEAD_DIM)


def _pack_ffn(w_in, w_out):
    d, f2 = w_in.shape
    f = f2 // 2
    fp = _round_up(f, FFN_TF)
    zc = jnp.zeros((d, fp - f), BF16)
    zr = jnp.zeros((fp - f, w_out.shape[1]), BF16)
    return (jnp.concatenate([w_in[:, :f].astype(BF16), zc], axis=1),
            jnp.concatenate([w_in[:, f:].astype(BF16), zc], axis=1),
            jnp.concatenate([w_out.astype(BF16), zr], axis=0))


def _pack_proj(w_in, d_mix):
    d = w_in.shape[0]
    c_conv = d_mix // C_CONV_FRACTION
    d_attn = d_mix - c_conv
    n_heads = d_attn // HEAD_DIM
    kvw = N_KV_HEADS * HEAD_DIM
    sizes = [2 * c_conv, d_attn, kvw, kvw, H_IDX * D_IDX, D_IDX, H_IDX]
    w_in = w_in.astype(BF16)
    parts, o = [], 0
    for sz in sizes:
        parts.append(w_in[:, o:o + sz])
        o += sz
    p_conv, q, k, v, qi, ki, wi = parts
    qh = q.reshape(d, n_heads, HEAD_DIM)
    zero = jnp.zeros_like(qh)
    odd = ((jnp.arange(n_heads) // (n_heads // N_KV_HEADS)) % 2 == 1)[None, :, None]
    qpad = jnp.concatenate([jnp.where(odd, zero, qh), jnp.where(odd, qh, zero)], axis=-1).reshape(d, n_heads * LANES)
    z = jnp.zeros((d, LANES - D_IDX), w_in.dtype)
    cols = [p_conv[:, :c_conv], p_conv[:, c_conv:], qpad, k, v, qi,
            jnp.concatenate([ki, z], axis=1), jnp.concatenate([z, ki], axis=1),
            jnp.concatenate([wi, jnp.zeros((d, LANES - H_IDX), w_in.dtype)], axis=1)]
    w_all = jnp.concatenate(cols, axis=1)
    assert w_all.shape[1] == _proj_layout(d_mix)[1]
    return w_all


def _own_half(o_pad, n_heads):
    odd = ((jnp.arange(n_heads) // (n_heads // N_KV_HEADS)) % 2 == 1)[:, None]
    return jnp.where(odd, o_pad[..., HEAD_DIM:], o_pad[..., :HEAD_DIM])


def kernel(x_prompt, x_sample, cache_k, cache_v, cache_idx_k, state_conv, page_table, norm_ffn1, ffn1_w_in, ffn1_w_out, norm_mix, w_in, conv_w, conv_b, conv_ln_g, conv_ln_b, w_out, norm_ffn2, ffn2_w_in, ffn2_w_out, norm_final):
    depth = w_in.shape[0]
    bsz, seq, d = x_prompt.shape
    db, dseq, _ = x_sample.shape
    assert dseq == 1
    d_mix = d
    c_conv = d_mix // C_CONV_FRACTION
    n_heads = (d_mix - c_conv) // HEAD_DIM
    page = cache_k.shape[2]
    n_pages = page_table.shape[1]
    past = n_pages * page
    n_sel_p = min(TOPK_MAX, seq // 4)
    n_sel_s = min(TOPK_MAX, (past + dseq) // 4)

    xp = x_prompt.reshape(bsz * seq, d)
    xs = x_sample.reshape(db, d)
    g_final = norm_final.reshape(1, d)
    outs_p, outs_s = [], []

    for l in range(depth):
        f1 = _pack_ffn(ffn1_w_in[l], ffn1_w_out[l])
        f2 = _pack_ffn(ffn2_w_in[l], ffn2_w_out[l])
        w_all = _pack_proj(w_in[l], d_mix)
        conv_p = jnp.stack([conv_b[l], conv_ln_g[l], conv_ln_b[l]])
        wo = w_out[l].astype(BF16)
        woc, woa = wo[:c_conv], wo[c_conv:]
        g1, gm, g2 = norm_ffn1[l].reshape(1, d), norm_mix[l].reshape(1, d), norm_ffn2[l].reshape(1, d)
        last = g_final if l == depth - 1 else None

        xp = _ffn(xp, g1, *f1, name="ffn1_prompt", tm_max=FFN1_TM)
        (conv_o, qpad, k_new, v_new, kb, vb, qi, kia, kib, ki_new, wi, u_tail) = _proj(
            xp, gm, w_all, conv_w[l], conv_p, seq_len=seq, name="proj_prompt")
        r3 = lambda a: a.reshape(bsz, seq, a.shape[-1])
        attn = _prompt_attention(r3(qi), r3(wi), r3(kia), r3(kib), r3(qpad), r3(kb), r3(vb), n_sel=n_sel_p)
        xp = _ffn(xp, g2, *f2, mix=(conv_o, attn.reshape(bsz * seq, -1), woc, woa), g_final=last,
                  name="ffn2_prompt")
        outs_p.append((k_new.reshape(bsz, seq, N_KV_HEADS, HEAD_DIM), v_new.reshape(bsz, seq, N_KV_HEADS, HEAD_DIM),
                       ki_new.reshape(bsz, seq, D_IDX), u_tail[:, CONV_HALO - (CONV_WIDTH - 1):, :]))

        state = state_conv[l].astype(F32)
        xs = _ffn(xs, g1, *f1, name="ffn1_sample")
        (conv_o, qpad, k_new, v_new, _, _, qi, _, _, ki_new, wi, u_new) = _proj(
            xs, gm, w_all, conv_w[l], conv_p, seq_len=1, state=jnp.swapaxes(state, 0, 1), name="proj_sample")
        mask = _sample_select(page_table, qi.reshape(db, H_IDX, D_IDX), wi.reshape(db, H_IDX, 1), ki_new,
                              cache_idx_k, layer=l, n_sel=n_sel_s)
        attn = _sample_attend(page_table, mask, _own_half(qpad.reshape(db, n_heads, LANES), n_heads), k_new, v_new,
                              cache_k, cache_v, layer=l).astype(BF16)
        xs = _ffn(xs, g2, *f2, mix=(conv_o, attn, woc, woa), g_final=last, name="ffn2_sample")
        outs_s.append((k_new.reshape(db, 1, N_KV_HEADS, HEAD_DIM), v_new.reshape(db, 1, N_KV_HEADS, HEAD_DIM),
                       ki_new.reshape(db, 1, D_IDX),
                       jnp.concatenate([state[:, 1:, :], u_new[:, None, :]], axis=1)))

    stack = lambda outs, i: jnp.stack([o[i] for o in outs])
    return (xp.reshape(bsz, seq, d), xs.reshape(db, 1, d),
            stack(outs_p, 0), stack(outs_p, 1), stack(outs_p, 2), stack(outs_p, 3),
            stack(outs_s, 0), stack(outs_s, 1), stack(outs_s, 2), stack(outs_s, 3))
```
